```python
import math
import jax, jax.numpy as jnp
from jax import lax
import numpy as np

D_MODEL = 1024
BATCH = 4
SEQ = 8192
DEPTH = 1
DEC_BATCH = 32
DEC_SEQ = 64
PAST_LEN = 1024

CHUNK = 64
N_HEADS = 8
HEAD_DIM = 64
V_DIM = 2 * HEAD_DIM
QK_COLS = N_HEADS * 2 * HEAD_DIM
ATTN_DIM = N_HEADS * V_DIM
CONV_DIM = D_MODEL
CONV_W = 3
D_FF = 2816
Q_BLOCK = 128
EPS = 1e-6
NEG = -1e30
SPLITS = [int(s) for s in np.cumsum([QK_COLS, QK_COLS, ATTN_DIM, CONV_DIM, CONV_DIM, CONV_DIM, D_MODEL])]
IN_COLS = 2 * QK_COLS + ATTN_DIM + 3 * CONV_DIM + 2 * D_MODEL

kernel_name = "hybrid_diffattn_shortconv_macaron_step"


def rmsnorm(x, g):
    xf = x.astype(jnp.float32)
    y = xf * lax.rsqrt(jnp.mean(xf * xf, axis=-1, keepdims=True) + EPS)
    return (y * g.astype(jnp.float32)).astype(x.dtype)


def swiglu(x, w_gate, w_up, w_down):
    return (jax.nn.silu(x @ w_gate) * (x @ w_up)) @ w_down


def lambda_init_fn(layer_idx):
    return 0.8 - 0.6 * math.exp(-0.3 * layer_idx)


def diff_attn_core(q, k, v, lam, mask):
    s = jnp.einsum('bqhmd,bkhmd->bhmqk', q, k).astype(jnp.float32) * (HEAD_DIM ** -0.5)
    if mask is not None:
        s = jnp.where(mask, s, NEG)
    p = jax.nn.softmax(s, axis=-1)
    a = p[:, :, 0] - lam * p[:, :, 1]
    return jnp.einsum('bhqk,bkhv->bqhv', a.astype(v.dtype), v)


def prompt_attention(q, k, v, lam):
    Bn, S = q.shape[0], q.shape[1]
    kpos_chunk = jnp.arange(S) // CHUNK

    def block(i):
        start = i * Q_BLOCK
        qb = lax.dynamic_slice_in_dim(q, start, Q_BLOCK, axis=1)
        qchunk = (start + jnp.arange(Q_BLOCK)) // CHUNK
        mask = kpos_chunk[None, :] <= qchunk[:, None]
        return diff_attn_core(qb, k, v, lam, mask)

    out = lax.map(block, jnp.arange(S // Q_BLOCK))
    return jnp.moveaxis(out, 0, 1).reshape(Bn, S, N_HEADS, V_DIM)


def token_mix(h, layer_idx, conv_prev, past_k, past_v, w_in, b_gate, lq1, lk1, lq2, lk2,
              subln_g, w_attn_out, conv_w, conv_b, w_conv_out, w_mix_out):
    Bn, T, _ = h.shape
    z = h @ w_in
    q, k, v, cb, cc, ch, ga, gb = jnp.split(z, SPLITS, axis=-1)
    q = q.reshape(Bn, T, N_HEADS, 2, HEAD_DIM)
    k = k.reshape(Bn, T, N_HEADS, 2, HEAD_DIM)
    v = v.reshape(Bn, T, N_HEADS, V_DIM)
    lam_init = lambda_init_fn(layer_idx)
    lam = (jnp.exp(jnp.sum(lq1.astype(jnp.float32) * lk1.astype(jnp.float32)))
           - jnp.exp(jnp.sum(lq2.astype(jnp.float32) * lk2.astype(jnp.float32))) + lam_init)
    if past_k is None:
        o = prompt_attention(q, k, v, lam)
    else:
        k_all = jnp.concatenate([past_k, k], axis=1)
        v_all = jnp.concatenate([past_v, v], axis=1)
        o = diff_attn_core(q, k_all, v_all, lam, None)
    o = (rmsnorm(o, subln_g) * (1.0 - lam_init)).astype(h.dtype).reshape(Bn, T, ATTN_DIM)
    y_attn = o @ w_attn_out

    u = cc * ch
    if conv_prev is None:
        conv_prev = jnp.zeros((Bn, CONV_W - 1, CONV_DIM), u.dtype)
    padded = jnp.concatenate([conv_prev.astype(u.dtype), u], axis=1)
    zc = conv_b
    for j in range(CONV_W):
        zc = zc + conv_w[j] * padded[:, j:j + T]
    y_conv = (cb * zc) @ w_conv_out
    new_conv = padded[:, -(CONV_W - 1):]

    g = jax.nn.sigmoid((jnp.concatenate([ga, gb], axis=-1) + b_gate).astype(jnp.float32)).astype(h.dtype)
    g_a, g_b = g[..., :D_MODEL], g[..., D_MODEL:]
    y = (g_a * y_attn + g_b * y_conv) @ w_mix_out
    return y, k, v, new_conv


def setup_inputs(seed: int = 0) -> dict:
    key = jax.random.key(seed)
    ks = jax.random.split(key, 32)
    f32 = jnp.float32
    nrm = lambda k, shape, s: jax.random.normal(k, shape, f32) * s
    gain = lambda k, n: 1.0 + 0.02 * jax.random.normal(k, (DEPTH, n), f32)
    return {
        "x_prompt": nrm(ks[0], (BATCH, SEQ, D_MODEL), 1.0),
        "x_sample": nrm(ks[1], (DEC_BATCH, DEC_SEQ, D_MODEL), 1.0),
        "cache_k": nrm(ks[2], (DEPTH, DEC_BATCH, PAST_LEN, N_HEADS, 2, HEAD_DIM), 1.0),
        "cache_v": nrm(ks[3], (DEPTH, DEC_BATCH, PAST_LEN, N_HEADS, V_DIM), 1.0),
        "state_conv": nrm(ks[4], (DEPTH, DEC_BATCH, CONV_W - 1, CONV_DIM), 1.0),
        "ffn1_norm_pre": gain(ks[5], D_MODEL),
        "ffn1_norm_post": gain(ks[6], D_MODEL),
        "ffn1_w_gate": nrm(ks[7], (DEPTH, D_MODEL, D_FF), D_MODEL ** -0.5),
        "ffn1_w_up": nrm(ks[8], (DEPTH, D_MODEL, D_FF), D_MODEL ** -0.5),
        "ffn1_w_down": nrm(ks[9], (DEPTH, D_FF, D_MODEL), D_FF ** -0.5),
        "mix_norm_pre": gain(ks[10], D_MODEL),
        "mix_norm_post": gain(ks[11], D_MODEL),
        "w_in": nrm(ks[12], (DEPTH, D_MODEL, IN_COLS), D_MODEL ** -0.5),
        "b_gate": nrm(ks[13], (DEPTH, 2 * D_MODEL), 0.02),
        "lambda_q1": nrm(ks[14], (DEPTH, HEAD_DIM), 0.1),
        "lambda_k1": nrm(ks[15], (DEPTH, HEAD_DIM), 0.1),
        "lambda_q2": nrm(ks[16], (DEPTH, HEAD_DIM), 0.1),
        "lambda_k2": nrm(ks[17], (DEPTH, HEAD_DIM), 0.1),
        "subln_g": gain(ks[18], V_DIM),
        "w_attn_out": nrm(ks[19], (DEPTH, ATTN_DIM, D_MODEL), ATTN_DIM ** -0.5),
        "conv_w": nrm(ks[20], (DEPTH, CONV_W, CONV_DIM), CONV_W ** -0.5),
        "conv_b": nrm(ks[21], (DEPTH, CONV_DIM), 0.02),
        "w_conv_out": nrm(ks[22], (DEPTH, CONV_DIM, D_MODEL), CONV_DIM ** -0.5),
        "w_mix_out": nrm(ks[23], (DEPTH, D_MODEL, D_MODEL), D_MODEL ** -0.5),
        "ffn2_norm_pre": gain(ks[24], D_MODEL),
        "ffn2_norm_post": gain(ks[25], D_MODEL),
        "ffn2_w_gate": nrm(ks[26], (DEPTH, D_MODEL, D_FF), D_MODEL ** -0.5),
        "ffn2_w_up": nrm(ks[27], (DEPTH, D_MODEL, D_FF), D_MODEL ** -0.5),
        "ffn2_w_down": nrm(ks[28], (DEPTH, D_FF, D_MODEL), D_FF ** -0.5),
    }


def reference(x_prompt, x_sample, cache_k, cache_v, state_conv,
              ffn1_norm_pre, ffn1_norm_post, ffn1_w_gate, ffn1_w_up, ffn1_w_down,
              mix_norm_pre, mix_norm_post, w_in, b_gate,
              lambda_q1, lambda_k1, lambda_q2, lambda_k2, subln_g, w_attn_out,
              conv_w, conv_b, w_conv_out, w_mix_out,
              ffn2_norm_pre, ffn2_norm_post, ffn2_w_gate, ffn2_w_up, ffn2_w_down):

    def run_layer(x, l, conv_prev, past_k, past_v):
        h = rmsnorm(x, ffn1_norm_pre[l])
        x = x + 0.5 * rmsnorm(swiglu(h, ffn1_w_gate[l], ffn1_w_up[l], ffn1_w_down[l]), ffn1_norm_post[l])
        h = rmsnorm(x, mix_norm_pre[l])
        y, k, v, c = token_mix(h, l, conv_prev, past_k, past_v, w_in[l], b_gate[l],
                               lambda_q1[l], lambda_k1[l], lambda_q2[l], lambda_k2[l],
                               subln_g[l], w_attn_out[l], conv_w[l], conv_b[l],
                               w_conv_out[l], w_mix_out[l])
        x = x + rmsnorm(y, mix_norm_post[l])
        h = rmsnorm(x, ffn2_norm_pre[l])
        x = x + 0.5 * rmsnorm(swiglu(h, ffn2_w_gate[l], ffn2_w_up[l], ffn2_w_down[l]), ffn2_norm_post[l])
        return x, k, v, c

    xp, xs = x_prompt, x_sample
    kp_l, vp_l, cp_l, ks_l, vs_l, cs_l = [], [], [], [], [], []
    for l in range(DEPTH):
        xp, kp, vp, cp = run_layer(xp, l, None, None, None)
        xs, ksm, vsm, csm = run_layer(xs, l, state_conv[l], cache_k[l], cache_v[l])
        kp_l.append(kp); vp_l.append(vp); cp_l.append(cp)
        ks_l.append(ksm); vs_l.append(vsm); cs_l.append(csm)
    new_k_prompt = jnp.stack(kp_l)
    new_v_prompt = jnp.stack(vp_l)
    new_conv_prompt = jnp.stack(cp_l)
    new_k_sample = jnp.stack(ks_l)
    new_v_sample = jnp.stack(vs_l)
    new_conv_sample = jnp.stack(cs_l)
    return (xp, xs, new_k_prompt, new_v_prompt, new_conv_prompt, new_k_sample, new_v_sample, new_conv_sample)
```

```python
import functools
import math

import jax
import jax.numpy as jnp
from jax import lax
from jax.experimental import pallas as pl
from jax.experimental.pallas import tpu as pltpu

D_MODEL = 1024
N_HEADS = 8
HEAD_DIM = 64
V_DIM = 2 * HEAD_DIM
HEAD_COLS = 2 * HEAD_DIM
D_FF = 2816
CHUNK = 64
CONV_W = 3
EPS = 1e-6
NEG = -1e30
LOG2E = 1.4426950408889634
Q_SCALE = (HEAD_DIM ** -0.5) * LOG2E

V7X_VMEM_LIMIT_BYTES = 58 * 1024 * 1024
ROW_TILE = 512
FF_CHUNK = 256
ATTN_TILE = 512

F32 = jnp.float32
BF16 = jnp.bfloat16


def _lambda_init(layer_idx):
    return 0.8 - 0.6 * math.exp(-0.3 * layer_idx)


def _rms(x, g):
    return x * lax.rsqrt(jnp.mean(x * x, axis=-1, keepdims=True) + EPS) * g


def _resident(shape):
    zeros = (0,) * len(shape)
    return pl.BlockSpec(shape, lambda *_: zeros, pipeline_mode=pl.Buffered(1))


def _params(n_axes):
    return pltpu.CompilerParams(
        dimension_semantics=("arbitrary",) * n_axes,
        vmem_limit_bytes=V7X_VMEM_LIMIT_BYTES,
    )


def _ffn_kernel(x_ref, gpre_ref, gpost_ref, wg_ref, wu_ref, wd_ref, o_ref, a_scr):
    x = x_ref[...]
    h = _rms(x, gpre_ref[...]).astype(BF16)
    for c in range(0, D_FF, FF_CHUNK):
        g = jnp.dot(h, wg_ref[:, c:c + FF_CHUNK], preferred_element_type=F32)
        u = jnp.dot(h, wu_ref[:, c:c + FF_CHUNK], preferred_element_type=F32)
        a_scr[:, c:c + FF_CHUNK] = (g * jax.nn.sigmoid(g) * u).astype(BF16)
    y = jnp.dot(a_scr[...], wd_ref[...], preferred_element_type=F32)
    o_ref[...] = x + 0.5 * _rms(y, gpost_ref[...])


def _ffn(x, g_pre, g_post, w_gate, w_up, w_down):
    t = x.shape[0]
    row = pl.BlockSpec((ROW_TILE, D_MODEL), lambda i: (i, 0))
    return pl.pallas_call(
        _ffn_kernel,
        grid=(t // ROW_TILE,),
        in_specs=[row, _resident((1, D_MODEL)), _resident((1, D_MODEL)),
                  _resident((D_MODEL, D_FF)), _resident((D_MODEL, D_FF)), _resident((D_FF, D_MODEL))],
        out_specs=row,
        out_shape=jax.ShapeDtypeStruct((t, D_MODEL), F32),
        scratch_shapes=[pltpu.VMEM((ROW_TILE, D_FF), BF16)],
        compiler_params=_params(1),
        name="ffn",
    )(x, g_pre, g_post, w_gate, w_up, w_down)


def _inproj_kernel(x_ref, gpre_ref, w_ref, bg_ref, cw_ref, cb_ref, wco_ref, st_ref, *refs,
                   nb, lt, tiles_per_seq, emit_kv):
    if emit_kv:
        q_ref, k_ref, v_ref, kb_ref, vt_ref, ga_ref, gy_ref, nc_ref, carry_scr = refs
    else:
        q_ref, k_ref, v_ref, ga_ref, gy_ref, nc_ref, carry_scr = refs
    i = pl.program_id(0)
    h = _rms(x_ref[...], gpre_ref[...]).astype(BF16)

    def seg(j):
        return jnp.dot(h, w_ref[:, j * D_MODEL:(j + 1) * D_MODEL], preferred_element_type=F32)

    q_ref[...] = (seg(0) * Q_SCALE).astype(BF16)
    k = seg(1)
    k_ref[...] = k
    v = seg(2)
    v_ref[...] = v
    if emit_kv:
        kb_ref[...] = k.astype(BF16)
        for hh in range(N_HEADS):
            vt_ref[hh] = v[:, hh * V_DIM:(hh + 1) * V_DIM].T.astype(BF16)

    @pl.when(i % tiles_per_seq == 0)
    def _():
        carry_scr[...] = st_ref[...]

    gate_b = seg(3).reshape(nb, lt, D_MODEL)
    u = (seg(4) * seg(5)).reshape(nb, lt, D_MODEL)
    prev = carry_scr[...]
    prev0 = prev[:, 0:1, :]
    prev1 = prev[:, 1:2, :]
    pos = lax.broadcasted_iota(jnp.int32, u.shape, 1)
    back1 = jnp.where(pos == 0, prev1, pltpu.roll(u, 1, 1))
    back2 = jnp.where(pos == 0, prev0, jnp.where(pos == 1, prev1, pltpu.roll(u, 2, 1)))
    cw = cw_ref[...]
    zc = cb_ref[...] + cw[0:1, :] * back2 + cw[1:2, :] * back1 + cw[2:3, :] * u
    tail = u[:, lt - (CONV_W - 1):, :]
    carry_scr[...] = tail
    nc_ref[...] = tail
    c_in = (gate_b * zc).reshape(nb * lt, D_MODEL).astype(BF16)
    y_conv = jnp.dot(c_in, wco_ref[...], preferred_element_type=F32)

    bg = bg_ref[...]
    ga_ref[...] = jax.nn.sigmoid(seg(6) + bg[:, :D_MODEL]).astype(BF16)
    gy_ref[...] = (jax.nn.sigmoid(seg(7) + bg[:, D_MODEL:]) * y_conv).astype(BF16)


def _inproj(x, g_pre, w_in, b_gate, conv_w, conv_b, w_conv_out, state, *, seq_len, emit_kv):
    t = x.shape[0]
    n_seq = t // seq_len
    lt = min(seq_len, ROW_TILE)
    nb = ROW_TILE // lt
    tiles_per_seq = seq_len // lt
    n_tiles = t // ROW_TILE
    row = lambda dt: (pl.BlockSpec((ROW_TILE, D_MODEL), lambda i: (i, 0)),
                      jax.ShapeDtypeStruct((t, D_MODEL), dt))
    seq_block = pl.BlockSpec((nb, CONV_W - 1, D_MODEL), lambda i: (i // tiles_per_seq, 0, 0))
    outs = [row(BF16), row(F32), row(F32)]
    if emit_kv:
        outs.append(row(BF16))
        outs.append((pl.BlockSpec((None, N_HEADS, None, V_DIM, ROW_TILE),
                                  lambda i: (i // tiles_per_seq, 0, i % tiles_per_seq, 0, 0)),
                     jax.ShapeDtypeStruct((n_seq, N_HEADS, tiles_per_seq, V_DIM, ROW_TILE), BF16)))
    outs += [row(BF16), row(BF16),
             (seq_block, jax.ShapeDtypeStruct((n_seq, CONV_W - 1, D_MODEL), F32))]
    kern = functools.partial(_inproj_kernel, nb=nb, lt=lt, tiles_per_seq=tiles_per_seq, emit_kv=emit_kv)
    return pl.pallas_call(
        kern,
        grid=(n_tiles,),
        in_specs=[pl.BlockSpec((ROW_TILE, D_MODEL), lambda i: (i, 0)),
                  _resident((1, D_MODEL)), _resident((D_MODEL, 8 * D_MODEL)), _resident((1, 2 * D_MODEL)),
                  _resident((CONV_W, D_MODEL)), _resident((1, D_MODEL)), _resident((D_MODEL, D_MODEL)),
                  seq_block],
        out_specs=[o[0] for o in outs],
        out_shape=[o[1] for o in outs],
        scratch_shapes=[pltpu.VMEM((nb, CONV_W - 1, D_MODEL), F32)],
        compiler_params=_params(1),
        name="inproj",
    )(x, g_pre, w_in, b_gate, conv_w, conv_b, w_conv_out, state)


def _lambda(lq1_ref, lk1_ref, lq2_ref, lk2_ref, lam_init):
    a = jnp.sum(lq1_ref[...] * lk1_ref[...], axis=-1, keepdims=True)
    b = jnp.sum(lq2_ref[...] * lk2_ref[...], axis=-1, keepdims=True)
    return jnp.exp(a) - jnp.exp(b) + lam_init


def _split_maps(qb):
    lane = lax.broadcasted_iota(jnp.int32, qb.shape, 1)
    zero = jnp.zeros_like(qb)
    return jnp.concatenate([jnp.where(lane < HEAD_DIM, qb, zero),
                            jnp.where(lane >= HEAD_DIM, qb, zero)], axis=0)


def _subln(o, g, lam_init):
    return (o * lax.rsqrt(jnp.mean(o * o, axis=-1, keepdims=True) + EPS) * g) * (1.0 - lam_init)


def _prompt_attn_kernel(lq1_ref, lk1_ref, lq2_ref, lk2_ref, sg_ref, q_ref, k_ref, vt_ref, o_ref,
                        m_scr, l_scr, acc_scr, *, lam_init):
    tq = tk = ATTN_TILE
    qi = pl.program_id(2)
    qq = _split_maps(q_ref[...])
    m_scr[...] = jnp.full(m_scr.shape, NEG, F32)
    l_scr[...] = jnp.zeros(l_scr.shape, F32)
    acc_scr[...] = jnp.zeros(acc_scr.shape, F32)

    def step(ki, masked):
        start = pl.multiple_of(ki * tk, tk)
        kb = k_ref[pl.ds(start, tk), :]
        s = lax.dot_general(kb, qq, (((1,), (1,)), ((), ())), preferred_element_type=F32)
        if masked:
            kc = lax.broadcasted_iota(jnp.int32, s.shape, 0) // CHUNK
            qc = (lax.broadcasted_iota(jnp.int32, s.shape, 1) % tq) // CHUNK
            s = jnp.where(kc <= qc, s, NEG)
        m_prev = m_scr[...]
        m_new = jnp.maximum(m_prev, jnp.max(s, axis=0, keepdims=True))
        alpha = jnp.exp2(m_prev - m_new)
        p = jnp.exp2(s - m_new)
        l_scr[...] = alpha * l_scr[...] + jnp.sum(p, axis=0, keepdims=True)
        pv = jnp.dot(vt_ref[ki], p.astype(BF16), preferred_element_type=F32)
        acc_scr[...] = alpha * acc_scr[...] + pv
        m_scr[...] = m_new

    def body(ki, carry):
        step(ki, False)
        return carry

    lax.fori_loop(0, qi, body, 0)
    step(qi, True)

    lam = _lambda(lq1_ref, lk1_ref, lq2_ref, lk2_ref, lam_init)
    accn = acc_scr[...] * (1.0 / l_scr[...])
    o_t = accn[:, :tq] - lam * accn[:, tq:]
    o_ref[...] = _subln(o_t.T, sg_ref[...], lam_init).astype(BF16)


def _prompt_attn(lq1, lk1, lq2, lk2, subln_g, q, kb, vt, *, n_seq, seq_len, lam_init):
    t = n_seq * seq_len
    nq = seq_len // ATTN_TILE
    vec = _resident((1, HEAD_DIM))
    return pl.pallas_call(
        functools.partial(_prompt_attn_kernel, lam_init=lam_init),
        grid=(n_seq, N_HEADS, nq),
        in_specs=[vec, vec, vec, vec, _resident((1, V_DIM)),
                  pl.BlockSpec((ATTN_TILE, HEAD_COLS), lambda b, h, i: (b * nq + i, h)),
                  pl.BlockSpec((seq_len, HEAD_COLS), lambda b, h, i: (b, h)),
                  pl.BlockSpec((None, None, nq, V_DIM, ATTN_TILE), lambda b, h, i: (b, h, 0, 0, 0))],
        out_specs=pl.BlockSpec((ATTN_TILE, V_DIM), lambda b, h, i: (b * nq + i, h)),
        out_shape=jax.ShapeDtypeStruct((t, N_HEADS * V_DIM), BF16),
        scratch_shapes=[pltpu.VMEM((1, 2 * ATTN_TILE), F32), pltpu.VMEM((1, 2 * ATTN_TILE), F32),
                        pltpu.VMEM((V_DIM, 2 * ATTN_TILE), F32)],
        compiler_params=_params(3),
        name="prompt_attn",
    )(lq1, lk1, lq2, lk2, subln_g, q, kb, vt)


def _sample_attn_kernel(lq1_ref, lk1_ref, lq2_ref, lk2_ref, sg_ref, q_ref, kn_ref, vn_ref, ck_ref, cv_ref,
                        o_ref, *, lam_init):
    tq = q_ref.shape[0]
    lam = _lambda(lq1_ref, lk1_ref, lq2_ref, lk2_ref, lam_init)
    nt = (((1,), (1,)), ((), ()))
    for hh in range(N_HEADS):
        cols = slice(hh * HEAD_COLS, (hh + 1) * HEAD_COLS)
        qq = _split_maps(q_ref[:, cols])
        s_c = lax.dot_general(qq, ck_ref[:, cols].astype(BF16), nt, preferred_element_type=F32)
        s_n = lax.dot_general(qq, kn_ref[:, cols].astype(BF16), nt, preferred_element_type=F32)
        m = jnp.maximum(jnp.max(s_c, axis=-1, keepdims=True), jnp.max(s_n, axis=-1, keepdims=True))
        p_c = jnp.exp2(s_c - m)
        p_n = jnp.exp2(s_n - m)
        inv = 1.0 / (jnp.sum(p_c, axis=-1, keepdims=True) + jnp.sum(p_n, axis=-1, keepdims=True))
        p_c = p_c * inv
        p_n = p_n * inv
        a_c = (p_c[:tq] - lam * p_c[tq:]).astype(BF16)
        a_n = (p_n[:tq] - lam * p_n[tq:]).astype(BF16)
        o = (jnp.dot(a_c, cv_ref[:, cols].astype(BF16), preferred_element_type=F32)
             + jnp.dot(a_n, vn_ref[:, cols].astype(BF16), preferred_element_type=F32))
        o_ref[:, cols] = _subln(o, sg_ref[...], lam_init).astype(BF16)


def _sample_attn(lq1, lk1, lq2, lk2, subln_g, q, k_new, v_new, cache_k, cache_v, *, n_seq, seq_len, lam_init):
    past = cache_k.shape[1]
    vec = _resident((1, HEAD_DIM))
    new = pl.BlockSpec((seq_len, D_MODEL), lambda b: (b, 0))
    old = pl.BlockSpec((None, past, D_MODEL), lambda b: (b, 0, 0))
    return pl.pallas_call(
        functools.partial(_sample_attn_kernel, lam_init=lam_init),
        grid=(n_seq,),
        in_specs=[vec, vec, vec, vec, _resident((1, V_DIM)), new, new, new, old, old],
        out_specs=new,
        out_shape=jax.ShapeDtypeStruct((n_seq * seq_len, D_MODEL), BF16),
        compiler_params=_params(1),
        name="sample_attn",
    )(lq1, lk1, lq2, lk2, subln_g, q, k_new, v_new, cache_k, cache_v)


def _mix_kernel(o_ref, ga_ref, gy_ref, x_ref, wa_ref, wm_ref, gpost_ref, out_ref):
    y_attn = jnp.dot(o_ref[...], wa_ref[...], preferred_element_type=F32)
    merged = (ga_ref[...].astype(F32) * y_attn + gy_ref[...].astype(F32)).astype(BF16)
    y = jnp.dot(merged, wm_ref[...], preferred_element_type=F32)
    out_ref[...] = x_ref[...] + _rms(y, gpost_ref[...])


def _mix(o, gate_a, gated_conv, x, w_attn_out, w_mix_out, g_post):
    t = x.shape[0]
    row = pl.BlockSpec((ROW_TILE, D_MODEL), lambda i: (i, 0))
    return pl.pallas_call(
        _mix_kernel,
        grid=(t // ROW_TILE,),
        in_specs=[row, row, row, row, _resident((D_MODEL, D_MODEL)), _resident((D_MODEL, D_MODEL)),
                  _resident((1, D_MODEL))],
        out_specs=row,
        out_shape=jax.ShapeDtypeStruct((t, D_MODEL), F32),
        compiler_params=_params(1),
        name="mix",
    )(o, gate_a, gated_conv, x, w_attn_out, w_mix_out, g_post)


def kernel(x_prompt, x_sample, cache_k, cache_v, state_conv, ffn1_norm_pre, ffn1_norm_post, ffn1_w_gate, ffn1_w_up, ffn1_w_down, mix_norm_pre, mix_norm_post, w_in, b_gate, lambda_q1, lambda_k1, lambda_q2, lambda_k2, subln_g, w_attn_out, conv_w, conv_b, w_conv_out, w_mix_out, ffn2_norm_pre, ffn2_norm_post, ffn2_w_gate, ffn2_w_up, ffn2_w_down):
    depth = w_in.shape[0]
    n_p, s_p, _ = x_prompt.shape
    n_s, s_s, _ = x_sample.shape
    past = cache_k.shape[2]
    assert s_p % ROW_TILE == 0 and ROW_TILE == ATTN_TILE and ATTN_TILE % CHUNK == 0
    assert ROW_TILE % s_s == 0 and (n_s * s_s) % ROW_TILE == 0 and s_s >= CONV_W - 1

    xp = x_prompt.reshape(n_p * s_p, D_MODEL)
    xs = x_sample.reshape(n_s * s_s, D_MODEL)
    outs = {name: [] for name in ("kp", "vp", "cp", "ks", "vs", "cs")}
    for l in range(depth):
        lam_init = _lambda_init(l)
        vec = lambda a: a[l][None, :]
        bf = lambda a: a[l].astype(BF16)
        ffn1 = (vec(ffn1_norm_pre), vec(ffn1_norm_post), bf(ffn1_w_gate), bf(ffn1_w_up), bf(ffn1_w_down))
        ffn2 = (vec(ffn2_norm_pre), vec(ffn2_norm_post), bf(ffn2_w_gate), bf(ffn2_w_up), bf(ffn2_w_down))
        proj = (vec(mix_norm_pre), bf(w_in), vec(b_gate), conv_w[l], vec(conv_b), bf(w_conv_out))
        lam = (vec(lambda_q1), vec(lambda_k1), vec(lambda_q2), vec(lambda_k2), vec(subln_g))
        mix_w = (bf(w_attn_out), bf(w_mix_out), vec(mix_norm_post))

        xp = _ffn(xp, *ffn1)
        zero_state = jnp.zeros((n_p, CONV_W - 1, D_MODEL), F32)
        q, k, v, kb, vt, ga, gy, conv = _inproj(xp, *proj, zero_state, seq_len=s_p, emit_kv=True)
        o = _prompt_attn(*lam, q, kb, vt, n_seq=n_p, seq_len=s_p, lam_init=lam_init)
        xp = _mix(o, ga, gy, xp, *mix_w)
        xp = _ffn(xp, *ffn2)
        outs["kp"].append(k.reshape(n_p, s_p, N_HEADS, 2, HEAD_DIM))
        outs["vp"].append(v.reshape(n_p, s_p, N_HEADS, V_DIM))
        outs["cp"].append(conv)

        xs = _ffn(xs, *ffn1)
        q, k, v, ga, gy, conv = _inproj(xs, *proj, state_conv[l], seq_len=s_s, emit_kv=False)
        o = _sample_attn(*lam, q, k, v, cache_k[l].reshape(n_s, past, D_MODEL),
                         cache_v[l].reshape(n_s, past, D_MODEL), n_seq=n_s, seq_len=s_s, lam_init=lam_init)
        xs = _mix(o, ga, gy, xs, *mix_w)
        xs = _ffn(xs, *ffn2)
        outs["ks"].append(k.reshape(n_s, s_s, N_HEADS, 2, HEAD_DIM))
        outs["vs"].append(v.reshape(n_s, s_s, N_HEADS, V_DIM))
        outs["cs"].append(conv)

    return (xp.reshape(n_p, s_p, D_MODEL), xs.reshape(n_s, s_s, D_MODEL),
            jnp.stack(outs["kp"]), jnp.stack(outs["vp"]), jnp.stack(outs["cp"]),
            jnp.stack(outs["ks"]), jnp.stack(outs["vs"]), jnp.stack(outs["cs"]))
```

```python
import functools
import math

import jax
import jax.numpy as jnp
from jax import lax
from jax.experimental import pallas as pl
from jax.experimental.pallas import tpu as pltpu

D_MODEL = 1024
N_HEADS = 8
HEAD_DIM = 64
V_DIM = 2 * HEAD_DIM
HEAD_COLS = 2 * HEAD_DIM
D_FF = 2816
CHUNK = 64
CONV_W = 3
EPS = 1e-6
NEG = -1e30
LOG2E = 1.4426950408889634
Q_SCALE = (HEAD_DIM ** -0.5) * LOG2E

V7X_VMEM_LIMIT_BYTES = 58 * 1024 * 1024
ROW_TILE = 512
MXU_COLS = 256
FF_CHUNK = MXU_COLS
ATTN_TQ = 512
ATTN_TK = 256

F32 = jnp.float32
BF16 = jnp.bfloat16


def _lambda_init(layer_idx):
    return 0.8 - 0.6 * math.exp(-0.3 * layer_idx)


def _rms(x, g):
    return x * lax.rsqrt(jnp.mean(x * x, axis=-1, keepdims=True) + EPS) * g


def _resident(shape):
    zeros = (0,) * len(shape)
    return pl.BlockSpec(shape, lambda *_: zeros, pipeline_mode=pl.Buffered(1))


def _params(n_axes, flags=None):
    return pltpu.CompilerParams(
        dimension_semantics=("arbitrary",) * n_axes,
        vmem_limit_bytes=V7X_VMEM_LIMIT_BYTES,
        flags=flags,
    )


def _ffn_kernel(x_ref, gpre_ref, gpost_ref, wg_ref, wu_ref, wd_ref, o_ref, a_scr):
    x = x_ref[...]
    h = _rms(x, gpre_ref[...]).astype(BF16)
    for c in range(0, D_FF, FF_CHUNK):
        g = jnp.dot(h, wg_ref[:, c:c + FF_CHUNK], preferred_element_type=F32)
        u = jnp.dot(h, wu_ref[:, c:c + FF_CHUNK], preferred_element_type=F32)
        a_scr[:, c:c + FF_CHUNK] = (g * jax.nn.sigmoid(g) * u).astype(BF16)
    y = jnp.dot(a_scr[...], wd_ref[...], preferred_element_type=F32)
    o_ref[...] = x + 0.5 * _rms(y, gpost_ref[...])


def _ffn(x, g_pre, g_post, w_gate, w_up, w_down):
    t = x.shape[0]
    row = pl.BlockSpec((ROW_TILE, D_MODEL), lambda i: (i, 0))
    return pl.pallas_call(
        _ffn_kernel,
        grid=(t // ROW_TILE,),
        in_specs=[row, _resident((1, D_MODEL)), _resident((1, D_MODEL)),
                  _resident((D_MODEL, D_FF)), _resident((D_MODEL, D_FF)), _resident((D_FF, D_MODEL))],
        out_specs=row,
        out_shape=jax.ShapeDtypeStruct((t, D_MODEL), F32),
        scratch_shapes=[pltpu.VMEM((ROW_TILE, D_FF), BF16)],
        compiler_params=_params(1),
        name="ffn",
    )(x, g_pre, g_post, w_gate, w_up, w_down)


def _inproj_kernel(x_ref, gpre_ref, w_ref, bg_ref, cw_ref, cb_ref, wco_ref, st_ref, *refs,
                   nb, lt, tiles_per_seq, emit_kv):
    if emit_kv:
        q_ref, k_ref, v_ref, kb_ref, vt_ref, ga_ref, gy_ref, nc_ref, carry_scr = refs
    else:
        q_ref, k_ref, v_ref, ga_ref, gy_ref, nc_ref, carry_scr = refs
    i = pl.program_id(0)
    h = _rms(x_ref[...], gpre_ref[...]).astype(BF16)

    def seg(j):
        return jnp.dot(h, w_ref[:, j * D_MODEL:(j + 1) * D_MODEL], preferred_element_type=F32)

    q_ref[...] = (seg(0) * Q_SCALE).astype(BF16)
    k = seg(1)
    k_ref[...] = k.T if emit_kv else k
    v = seg(2)
    v_ref[...] = v
    if emit_kv:
        kb_ref[...] = k.astype(BF16)
        for hh in range(N_HEADS):
            for kb in range(ROW_TILE // ATTN_TK):
                vt_ref[hh, kb] = v[kb * ATTN_TK:(kb + 1) * ATTN_TK, hh * V_DIM:(hh + 1) * V_DIM].T.astype(BF16)

    @pl.when(i % tiles_per_seq == 0)
    def _():
        carry_scr[...] = st_ref[...]

    gate_b = seg(3).reshape(nb, lt, D_MODEL)
    u = (seg(4) * seg(5)).reshape(nb, lt, D_MODEL)
    prev = carry_scr[...]
    prev0 = prev[:, 0:1, :]
    prev1 = prev[:, 1:2, :]
    pos = lax.broadcasted_iota(jnp.int32, u.shape, 1)
    back1 = jnp.where(pos == 0, prev1, pltpu.roll(u, 1, 1))
    back2 = jnp.where(pos == 0, prev0, jnp.where(pos == 1, prev1, pltpu.roll(u, 2, 1)))
    cw = cw_ref[...]
    zc = cb_ref[...] + cw[0:1, :] * back2 + cw[1:2, :] * back1 + cw[2:3, :] * u
    tail = u[:, lt - (CONV_W - 1):, :]
    carry_scr[...] = tail
    nc_ref[...] = tail
    c_in = (gate_b * zc).reshape(nb * lt, D_MODEL).astype(BF16)
    y_conv = jnp.dot(c_in, wco_ref[...], preferred_element_type=F32)

    bg = bg_ref[...]
    ga_ref[...] = jax.nn.sigmoid(seg(6) + bg[:, :D_MODEL]).astype(BF16)
    gy_ref[...] = (jax.nn.sigmoid(seg(7) + bg[:, D_MODEL:]) * y_conv).astype(BF16)


def _inproj(x, g_pre, w_in, b_gate, conv_w, conv_b, w_conv_out, state, *, seq_len, emit_kv):
    t = x.shape[0]
    n_seq = t // seq_len
    lt = min(seq_len, ROW_TILE)
    nb = ROW_TILE // lt
    tiles_per_seq = seq_len // lt
    n_tiles = t // ROW_TILE
    row = lambda dt: (pl.BlockSpec((ROW_TILE, D_MODEL), lambda i: (i, 0)),
                      jax.ShapeDtypeStruct((t, D_MODEL), dt))
    seq_block = pl.BlockSpec((nb, CONV_W - 1, D_MODEL), lambda i: (i // tiles_per_seq, 0, 0))
    outs = [row(BF16), row(F32), row(F32)]
    if emit_kv:
        outs[1] = (pl.BlockSpec((None, D_MODEL, ROW_TILE), lambda i: (i // tiles_per_seq, 0, i % tiles_per_seq)),
                   jax.ShapeDtypeStruct((n_seq, D_MODEL, seq_len), F32))
        outs.append(row(BF16))
        kb_per_tile = ROW_TILE // ATTN_TK
        outs.append((pl.BlockSpec((None, N_HEADS, kb_per_tile, V_DIM, ATTN_TK),
                                  lambda i: (i // tiles_per_seq, 0, i % tiles_per_seq, 0, 0)),
                     jax.ShapeDtypeStruct((n_seq, N_HEADS, tiles_per_seq * kb_per_tile, V_DIM, ATTN_TK), BF16)))
    outs += [row(BF16), row(BF16),
             (seq_block, jax.ShapeDtypeStruct((n_seq, CONV_W - 1, D_MODEL), F32))]
    kern = functools.partial(_inproj_kernel, nb=nb, lt=lt, tiles_per_seq=tiles_per_seq, emit_kv=emit_kv)
    return pl.pallas_call(
        kern,
        grid=(n_tiles,),
        in_specs=[pl.BlockSpec((ROW_TILE, D_MODEL), lambda i: (i, 0)),
                  _resident((1, D_MODEL)), _resident((D_MODEL, 8 * D_MODEL)), _resident((1, 2 * D_MODEL)),
                  _resident((CONV_W, D_MODEL)), _resident((1, D_MODEL)), _resident((D_MODEL, D_MODEL)),
                  seq_block],
        out_specs=[o[0] for o in outs],
        out_shape=[o[1] for o in outs],
        scratch_shapes=[pltpu.VMEM((nb, CONV_W - 1, D_MODEL), F32)],
        compiler_params=_params(1),
        name="inproj",
    )(x, g_pre, w_in, b_gate, conv_w, conv_b, w_conv_out, state)


def _lambda(lq1_ref, lk1_ref, lq2_ref, lk2_ref, lam_init):
    a = jnp.sum(lq1_ref[...] * lk1_ref[...], axis=-1, keepdims=True)
    b = jnp.sum(lq2_ref[...] * lk2_ref[...], axis=-1, keepdims=True)
    return jnp.exp(a) - jnp.exp(b) + lam_init


def _split_maps(qb):
    lane = lax.broadcasted_iota(jnp.int32, qb.shape, 1)
    zero = jnp.zeros_like(qb)
    return jnp.concatenate([jnp.where(lane < HEAD_DIM, qb, zero),
                            jnp.where(lane >= HEAD_DIM, qb, zero)], axis=0)


def _subln(o, g, lam_init):
    return (o * lax.rsqrt(jnp.mean(o * o, axis=-1, keepdims=True) + EPS) * g) * (1.0 - lam_init)


def _prompt_attn_kernel(lq1_ref, lk1_ref, lq2_ref, lk2_ref, sg_ref, q_ref, k_ref, vt_ref, o_ref,
                        qq_scr, s_a, s_b, m_scr, l_scr, acc_scr, *, lam_init):
    tq, tk = ATTN_TQ, ATTN_TK
    qi = pl.program_id(2)
    q_t = q_ref[...].astype(F32).T
    row = lax.broadcasted_iota(jnp.int32, q_t.shape, 0)
    qq_scr[:, :tq] = jnp.where(row < HEAD_DIM, q_t, 0.0).astype(BF16)
    qq_scr[:, tq:] = jnp.where(row >= HEAD_DIM, q_t, 0.0).astype(BF16)
    m_scr[...] = jnp.full(m_scr.shape, NEG, F32)
    l_scr[...] = jnp.zeros(l_scr.shape, F32)
    acc_scr[...] = jnp.zeros(acc_scr.shape, F32)

    all_tiles = tuple(range(0, 2 * tq, MXU_COLS))

    def softmax(s_in, c, key_offset):
        strip = slice(c, c + 128)
        s = s_in[:, strip]
        if key_offset is not None:
            kc = (lax.broadcasted_iota(jnp.int32, s.shape, 0) + key_offset) // CHUNK
            qc = (lax.broadcasted_iota(jnp.int32, s.shape, 1) + (c % tq)) // CHUNK
            s = jnp.where(kc <= qc, s, NEG)
        m_prev = m_scr[:, strip]
        m_new = jnp.maximum(m_prev, jnp.max(s, axis=0, keepdims=True))
        alpha = jnp.exp2(m_prev - m_new)
        p = jnp.exp2(s - m_new)
        l_scr[:, strip] = alpha * l_scr[:, strip] + jnp.sum(p, axis=0, keepdims=True)
        m_scr[:, strip] = m_new
        return p.astype(BF16), alpha

    def phase(next_blk, s_next, cur_blk, s_cur, key_offset=None, next_tiles=all_tiles, cur_tiles=all_tiles):
        if next_blk is not None:
            k_blk = k_ref[pl.ds(pl.multiple_of(next_blk * tk, tk), tk), :]
        if cur_blk is not None:
            v_blk = vt_ref[cur_blk]
        for c in all_tiles:
            cols = slice(c, c + MXU_COLS)
            if next_blk is not None and c in next_tiles:
                s_next[:, cols] = jnp.dot(k_blk, qq_scr[:, cols], preferred_element_type=F32)
            if cur_blk is not None and c in cur_tiles:
                strips = [softmax(s_cur, cc, key_offset) for cc in range(c, c + MXU_COLS, 128)]
                p = jnp.concatenate([st[0] for st in strips], axis=1)
                alpha = jnp.concatenate([st[1] for st in strips], axis=1)
                pv = jnp.dot(v_blk, p, preferred_element_type=F32)
                acc_scr[:, cols] = alpha * acc_scr[:, cols] + pv

    def pair(first_blk):
        phase(first_blk + 1, s_b, first_blk, s_a)
        phase(first_blk + 2, s_a, first_blk + 1, s_b)

    phase(0, s_a, None, None)

    def body(t, carry):
        pair(4 * t)
        pair(4 * t + 2)
        return carry

    lax.fori_loop(0, qi // 2, body, 0)

    @pl.when(qi % 2 == 1)
    def _():
        pair(2 * qi - 2)

    blk = 2 * qi
    late_tiles = tuple(c for c in all_tiles if c % tq >= tk)
    phase(blk + 1, s_b, blk, s_a, key_offset=0, next_tiles=late_tiles)
    phase(None, None, blk + 1, s_b, key_offset=tk, cur_tiles=late_tiles)

    lam = _lambda(lq1_ref, lk1_ref, lq2_ref, lk2_ref, lam_init)
    accn = acc_scr[...] * (1.0 / l_scr[...])
    o_t = accn[:, :tq] - lam * accn[:, tq:]
    o_ref[...] = _subln(o_t.T, sg_ref[...], lam_init).astype(BF16)


def _prompt_attn(lq1, lk1, lq2, lk2, subln_g, q, kb, vt, *, n_seq, seq_len, lam_init):
    t = n_seq * seq_len
    nq = seq_len // ATTN_TQ
    nk = seq_len // ATTN_TK
    vec = _resident((1, HEAD_DIM))
    stat = pltpu.VMEM((1, 2 * ATTN_TQ), F32)
    score = pltpu.VMEM((ATTN_TK, 2 * ATTN_TQ), F32)
    return pl.pallas_call(
        functools.partial(_prompt_attn_kernel, lam_init=lam_init),
        grid=(n_seq, N_HEADS, nq),
        in_specs=[vec, vec, vec, vec, _resident((1, V_DIM)),
                  pl.BlockSpec((ATTN_TQ, HEAD_COLS), lambda b, h, i: (b * nq + i, h)),
                  pl.BlockSpec((seq_len, HEAD_COLS), lambda b, h, i: (b, h)),
                  pl.BlockSpec((None, None, nk, V_DIM, ATTN_TK), lambda b, h, i: (b, h, 0, 0, 0))],
        out_specs=pl.BlockSpec((ATTN_TQ, V_DIM), lambda b, h, i: (b * nq + i, h)),
        out_shape=jax.ShapeDtypeStruct((t, N_HEADS * V_DIM), BF16),
        scratch_shapes=[pltpu.VMEM((HEAD_COLS, 2 * ATTN_TQ), BF16), score, score, stat, stat,
                        pltpu.VMEM((V_DIM, 2 * ATTN_TQ), F32)],
        compiler_params=_params(3),
        name="prompt_attn",
    )(lq1, lk1, lq2, lk2, subln_g, q, kb, vt)


def _sample_attn_kernel(lq1_ref, lk1_ref, lq2_ref, lk2_ref, sg_ref, q_ref, kn_ref, vn_ref, ck_ref, cv_ref,
                        o_ref, *, lam_init):
    tq = q_ref.shape[0]
    lam = _lambda(lq1_ref, lk1_ref, lq2_ref, lk2_ref, lam_init)
    nt = (((1,), (1,)), ((), ()))
    for hh in range(N_HEADS):
        cols = slice(hh * HEAD_COLS, (hh + 1) * HEAD_COLS)
        qq = _split_maps(q_ref[:, cols])
        s_c = lax.dot_general(qq, ck_ref[:, cols].astype(BF16), nt, preferred_element_type=F32)
        s_n = lax.dot_general(qq, kn_ref[:, cols].astype(BF16), nt, preferred_element_type=F32)
        m = jnp.maximum(jnp.max(s_c, axis=-1, keepdims=True), jnp.max(s_n, axis=-1, keepdims=True))
        p_c = jnp.exp2(s_c - m)
        p_n = jnp.exp2(s_n - m)
        inv = 1.0 / (jnp.sum(p_c, axis=-1, keepdims=True) + jnp.sum(p_n, axis=-1, keepdims=True))
        p_c = p_c * inv
        p_n = p_n * inv
        a_c = (p_c[:tq] - lam * p_c[tq:]).astype(BF16)
        a_n = (p_n[:tq] - lam * p_n[tq:]).astype(BF16)
        o = (jnp.dot(a_c, cv_ref[:, cols].astype(BF16), preferred_element_type=F32)
             + jnp.dot(a_n, vn_ref[:, cols].astype(BF16), preferred_element_type=F32))
        o_ref[:, cols] = _subln(o, sg_ref[...], lam_init).astype(BF16)


def _sample_attn(lq1, lk1, lq2, lk2, subln_g, q, k_new, v_new, cache_k, cache_v, *, n_seq, seq_len, lam_init):
    past = cache_k.shape[1]
    vec = _resident((1, HEAD_DIM))
    new = pl.BlockSpec((seq_len, D_MODEL), lambda b: (b, 0))
    old = pl.BlockSpec((None, past, D_MODEL), lambda b: (b, 0, 0))
    return pl.pallas_call(
        functools.partial(_sample_attn_kernel, lam_init=lam_init),
        grid=(n_seq,),
        in_specs=[vec, vec, vec, vec, _resident((1, V_DIM)), new, new, new, old, old],
        out_specs=new,
        out_shape=jax.ShapeDtypeStruct((n_seq * seq_len, D_MODEL), BF16),
        compiler_params=_params(1),
        name="sample_attn",
    )(lq1, lk1, lq2, lk2, subln_g, q, k_new, v_new, cache_k, cache_v)


def _mix_kernel(o_ref, ga_ref, gy_ref, x_ref, wa_ref, wm_ref, gpost_ref, out_ref):
    y_attn = jnp.dot(o_ref[...], wa_ref[...], preferred_element_type=F32)
    merged = (ga_ref[...].astype(F32) * y_attn + gy_ref[...].astype(F32)).astype(BF16)
    y = jnp.dot(merged, wm_ref[...], preferred_element_type=F32)
    out_ref[...] = x_ref[...] + _rms(y, gpost_ref[...])


def _mix(o, gate_a, gated_conv, x, w_attn_out, w_mix_out, g_post):
    t = x.shape[0]
    row = pl.BlockSpec((ROW_TILE, D_MODEL), lambda i: (i, 0))
    return pl.pallas_call(
        _mix_kernel,
        grid=(t // ROW_TILE,),
        in_specs=[row, row, row, row, _resident((D_MODEL, D_MODEL)), _resident((D_MODEL, D_MODEL)),
                  _resident((1, D_MODEL))],
        out_specs=row,
        out_shape=jax.ShapeDtypeStruct((t, D_MODEL), F32),
        compiler_params=_params(1),
        name="mix",
    )(o, gate_a, gated_conv, x, w_attn_out, w_mix_out, g_post)


def kernel(x_prompt, x_sample, cache_k, cache_v, state_conv, ffn1_norm_pre, ffn1_norm_post, ffn1_w_gate, ffn1_w_up, ffn1_w_down, mix_norm_pre, mix_norm_post, w_in, b_gate, lambda_q1, lambda_k1, lambda_q2, lambda_k2, subln_g, w_attn_out, conv_w, conv_b, w_conv_out, w_mix_out, ffn2_norm_pre, ffn2_norm_post, ffn2_w_gate, ffn2_w_up, ffn2_w_down):
    depth = w_in.shape[0]
    n_p, s_p, _ = x_prompt.shape
    n_s, s_s, _ = x_sample.shape
    past = cache_k.shape[2]
    assert s_p % ROW_TILE == 0 and ROW_TILE % ATTN_TK == 0 and ATTN_TQ == 2 * ATTN_TK and ATTN_TK % CHUNK == 0
    assert ROW_TILE % s_s == 0 and (n_s * s_s) % ROW_TILE == 0 and s_s >= CONV_W - 1

    xp = x_prompt.reshape(n_p * s_p, D_MODEL)
    xs = x_sample.reshape(n_s * s_s, D_MODEL)
    outs = {name: [] for name in ("kp", "vp", "cp", "ks", "vs", "cs")}
    for l in range(depth):
        lam_init = _lambda_init(l)
        vec = lambda a: a[l][None, :]
        bf = lambda a: a[l].astype(BF16)
        ffn1 = (vec(ffn1_norm_pre), vec(ffn1_norm_post), bf(ffn1_w_gate), bf(ffn1_w_up), bf(ffn1_w_down))
        ffn2 = (vec(ffn2_norm_pre), vec(ffn2_norm_post), bf(ffn2_w_gate), bf(ffn2_w_up), bf(ffn2_w_down))
        proj = (vec(mix_norm_pre), bf(w_in), vec(b_gate), conv_w[l], vec(conv_b), bf(w_conv_out))
        lam = (vec(lambda_q1), vec(lambda_k1), vec(lambda_q2), vec(lambda_k2), vec(subln_g))
        mix_w = (bf(w_attn_out), bf(w_mix_out), vec(mix_norm_post))

        xp = _ffn(xp, *ffn1)
        zero_state = jnp.zeros((n_p, CONV_W - 1, D_MODEL), F32)
        q, k, v, kb, vt, ga, gy, conv = _inproj(xp, *proj, zero_state, seq_len=s_p, emit_kv=True)
        o = _prompt_attn(*lam, q, kb, vt, n_seq=n_p, seq_len=s_p, lam_init=lam_init)
        xp = _mix(o, ga, gy, xp, *mix_w)
        xp = _ffn(xp, *ffn2)
        outs["kp"].append(jnp.transpose(k.reshape(n_p, N_HEADS, 2, HEAD_DIM, s_p), (0, 4, 1, 2, 3)))
        outs["vp"].append(v.reshape(n_p, s_p, N_HEADS, V_DIM))
        outs["cp"].append(conv)

        xs = _ffn(xs, *ffn1)
        q, k, v, ga, gy, conv = _inproj(xs, *proj, state_conv[l], seq_len=s_s, emit_kv=False)
        o = _sample_attn(*lam, q, k, v, cache_k[l].reshape(n_s, past, D_MODEL),
                         cache_v[l].reshape(n_s, past, D_MODEL), n_seq=n_s, seq_len=s_s, lam_init=lam_init)
        xs = _mix(o, ga, gy, xs, *mix_w)
        xs = _ffn(xs, *ffn2)
        outs["ks"].append(k.reshape(n_s, s_s, N_HEADS, 2, HEAD_DIM))
        outs["vs"].append(v.reshape(n_s, s_s, N_HEADS, V_DIM))
        outs["cs"].append(conv)

    return (xp.reshape(n_p, s_p, D_MODEL), xs.reshape(n_s, s_s, D_MODEL),
            jnp.stack(outs["kp"]), jnp.stack(outs["vp"]), jnp.stack(outs["cp"]),
            jnp.stack(outs["ks"]), jnp.stack(outs["vs"]), jnp.stack(outs["cs"]))
```

```python
import functools
import math

import jax
import jax.numpy as jnp
from jax import lax
from jax.experimental import pallas as pl
from jax.experimental.pallas import tpu as pltpu

D_MODEL = 1024
N_HEADS = 8
HEAD_DIM = 64
V_DIM = 2 * HEAD_DIM
HEAD_COLS = 2 * HEAD_DIM
D_FF = 2816
CHUNK = 64
CONV_W = 3
EPS = 1e-6
NEG = -1e30
LOG2E = 1.4426950408889634
Q_SCALE = (HEAD_DIM ** -0.5) * LOG2E

V7X_VMEM_LIMIT_BYTES = 58 * 1024 * 1024
ROW_TILE = 512
MXU_COLS = 256
FF_CHUNK = MXU_COLS
ATTN_TQ = 1024
ATTN_TK = 256
BF16_SUBLANES = 16
VT_ROWS = V_DIM + BF16_SUBLANES

F32 = jnp.float32
BF16 = jnp.bfloat16


def _lambda_init(layer_idx):
    return 0.8 - 0.6 * math.exp(-0.3 * layer_idx)


def _rms(x, g):
    return x * lax.rsqrt(jnp.mean(x * x, axis=-1, keepdims=True) + EPS) * g


def _resident(shape):
    zeros = (0,) * len(shape)
    return pl.BlockSpec(shape, lambda *_: zeros, pipeline_mode=pl.Buffered(1))


def _params(n_axes, flags=None):
    return pltpu.CompilerParams(
        dimension_semantics=("arbitrary",) * n_axes,
        vmem_limit_bytes=V7X_VMEM_LIMIT_BYTES,
        flags=flags,
    )


def _ffn_kernel(x_ref, gpre_ref, gpost_ref, wg_ref, wu_ref, wd_ref, o_ref, a_scr):
    x = x_ref[...]
    h = _rms(x, gpre_ref[...]).astype(BF16)
    for c in range(0, D_FF, FF_CHUNK):
        g = jnp.dot(h, wg_ref[:, c:c + FF_CHUNK], preferred_element_type=F32)
        u = jnp.dot(h, wu_ref[:, c:c + FF_CHUNK], preferred_element_type=F32)
        a_scr[:, c:c + FF_CHUNK] = (g * jax.nn.sigmoid(g) * u).astype(BF16)
    y = jnp.dot(a_scr[...], wd_ref[...], preferred_element_type=F32)
    o_ref[...] = x + 0.5 * _rms(y, gpost_ref[...])


def _ffn(x, g_pre, g_post, w_gate, w_up, w_down):
    t = x.shape[0]
    row = pl.BlockSpec((ROW_TILE, D_MODEL), lambda i: (i, 0))
    return pl.pallas_call(
        _ffn_kernel,
        grid=(t // ROW_TILE,),
        in_specs=[row, _resident((1, D_MODEL)), _resident((1, D_MODEL)),
                  _resident((D_MODEL, D_FF)), _resident((D_MODEL, D_FF)), _resident((D_FF, D_MODEL))],
        out_specs=row,
        out_shape=jax.ShapeDtypeStruct((t, D_MODEL), F32),
        scratch_shapes=[pltpu.VMEM((ROW_TILE, D_FF), BF16)],
        compiler_params=_params(1),
        name="ffn",
    )(x, g_pre, g_post, w_gate, w_up, w_down)


def _inproj_kernel(x_ref, gpre_ref, w_ref, bg_ref, cw_ref, cb_ref, wco_ref, st_ref, *refs,
                   nb, lt, tiles_per_seq, emit_kv):
    if emit_kv:
        q_ref, k_ref, v_ref, kb_ref, vt_ref, ga_ref, gy_ref, nc_ref, carry_scr = refs
    else:
        q_ref, k_ref, v_ref, ga_ref, gy_ref, nc_ref, carry_scr = refs
    i = pl.program_id(0)
    h = _rms(x_ref[...], gpre_ref[...]).astype(BF16)

    def seg(j):
        return jnp.dot(h, w_ref[:, j * D_MODEL:(j + 1) * D_MODEL], preferred_element_type=F32)

    q_ref[...] = (seg(0) * Q_SCALE).astype(BF16)
    k = seg(1)
    k_ref[...] = k.T if emit_kv else k
    v = seg(2)
    v_ref[...] = v
    if emit_kv:
        kb_ref[...] = k.astype(BF16)
        for hh in range(N_HEADS):
            for kb in range(ROW_TILE // ATTN_TK):
                vt_ref[hh, kb, :V_DIM, :] = (
                    v[kb * ATTN_TK:(kb + 1) * ATTN_TK, hh * V_DIM:(hh + 1) * V_DIM].T.astype(BF16))
                vt_ref[hh, kb, V_DIM:, :] = jnp.ones((BF16_SUBLANES, ATTN_TK), BF16)

    @pl.when(i % tiles_per_seq == 0)
    def _():
        carry_scr[...] = st_ref[...]

    gate_b = seg(3).reshape(nb, lt, D_MODEL)
    u = (seg(4) * seg(5)).reshape(nb, lt, D_MODEL)
    prev = carry_scr[...]
    prev0 = prev[:, 0:1, :]
    prev1 = prev[:, 1:2, :]
    pos = lax.broadcasted_iota(jnp.int32, u.shape, 1)
    back1 = jnp.where(pos == 0, prev1, pltpu.roll(u, 1, 1))
    back2 = jnp.where(pos == 0, prev0, jnp.where(pos == 1, prev1, pltpu.roll(u, 2, 1)))
    cw = cw_ref[...]
    zc = cb_ref[...] + cw[0:1, :] * back2 + cw[1:2, :] * back1 + cw[2:3, :] * u
    tail = u[:, lt - (CONV_W - 1):, :]
    carry_scr[...] = tail
    nc_ref[...] = tail
    c_in = (gate_b * zc).reshape(nb * lt, D_MODEL).astype(BF16)
    y_conv = jnp.dot(c_in, wco_ref[...], preferred_element_type=F32)

    bg = bg_ref[...]
    ga_ref[...] = jax.nn.sigmoid(seg(6) + bg[:, :D_MODEL]).astype(BF16)
    gy_ref[...] = (jax.nn.sigmoid(seg(7) + bg[:, D_MODEL:]) * y_conv).astype(BF16)


def _inproj(x, g_pre, w_in, b_gate, conv_w, conv_b, w_conv_out, state, *, seq_len, emit_kv):
    t = x.shape[0]
    n_seq = t // seq_len
    lt = min(seq_len, ROW_TILE)
    nb = ROW_TILE // lt
    tiles_per_seq = seq_len // lt
    n_tiles = t // ROW_TILE
    row = lambda dt: (pl.BlockSpec((ROW_TILE, D_MODEL), lambda i: (i, 0)),
                      jax.ShapeDtypeStruct((t, D_MODEL), dt))
    seq_block = pl.BlockSpec((nb, CONV_W - 1, D_MODEL), lambda i: (i // tiles_per_seq, 0, 0))
    outs = [row(BF16), row(F32), row(F32)]
    if emit_kv:
        outs[1] = (pl.BlockSpec((None, D_MODEL, ROW_TILE), lambda i: (i // tiles_per_seq, 0, i % tiles_per_seq)),
                   jax.ShapeDtypeStruct((n_seq, D_MODEL, seq_len), F32))
        outs.append(row(BF16))
        kb_per_tile = ROW_TILE // ATTN_TK
        outs.append((pl.BlockSpec((None, N_HEADS, kb_per_tile, VT_ROWS, ATTN_TK),
                                  lambda i: (i // tiles_per_seq, 0, i % tiles_per_seq, 0, 0)),
                     jax.ShapeDtypeStruct((n_seq, N_HEADS, tiles_per_seq * kb_per_tile, VT_ROWS, ATTN_TK), BF16)))
    outs += [row(BF16), row(BF16),
             (seq_block, jax.ShapeDtypeStruct((n_seq, CONV_W - 1, D_MODEL), F32))]
    kern = functools.partial(_inproj_kernel, nb=nb, lt=lt, tiles_per_seq=tiles_per_seq, emit_kv=emit_kv)
    return pl.pallas_call(
        kern,
        grid=(n_tiles,),
        in_specs=[pl.BlockSpec((ROW_TILE, D_MODEL), lambda i: (i, 0)),
                  _resident((1, D_MODEL)), _resident((D_MODEL, 8 * D_MODEL)), _resident((1, 2 * D_MODEL)),
                  _resident((CONV_W, D_MODEL)), _resident((1, D_MODEL)), _resident((D_MODEL, D_MODEL)),
                  seq_block],
        out_specs=[o[0] for o in outs],
        out_shape=[o[1] for o in outs],
        scratch_shapes=[pltpu.VMEM((nb, CONV_W - 1, D_MODEL), F32)],
        compiler_params=_params(1),
        name="inproj",
    )(x, g_pre, w_in, b_gate, conv_w, conv_b, w_conv_out, state)


def _lambda(lq1_ref, lk1_ref, lq2_ref, lk2_ref, lam_init):
    a = jnp.sum(lq1_ref[...] * lk1_ref[...], axis=-1, keepdims=True)
    b = jnp.sum(lq2_ref[...] * lk2_ref[...], axis=-1, keepdims=True)
    return jnp.exp(a) - jnp.exp(b) + lam_init


def _split_maps(qb):
    lane = lax.broadcasted_iota(jnp.int32, qb.shape, 1)
    zero = jnp.zeros_like(qb)
    return jnp.concatenate([jnp.where(lane < HEAD_DIM, qb, zero),
                            jnp.where(lane >= HEAD_DIM, qb, zero)], axis=0)


def _subln(o, g, lam_init):
    return (o * lax.rsqrt(jnp.mean(o * o, axis=-1, keepdims=True) + EPS) * g) * (1.0 - lam_init)


def _prompt_attn_kernel(lq1_ref, lk1_ref, lq2_ref, lk2_ref, sg_ref, q_ref, k_ref, vt_ref, o_ref,
                        qq_scr, s_a, s_b, m_scr, acc_scr, *, lam_init):
    tq, tk = ATTN_TQ, ATTN_TK
    qi = pl.program_id(2)
    q_t = q_ref[...].astype(F32).T
    row = lax.broadcasted_iota(jnp.int32, q_t.shape, 0)
    qq_scr[:, :tq] = jnp.where(row < HEAD_DIM, q_t, 0.0).astype(BF16)
    qq_scr[:, tq:] = jnp.where(row >= HEAD_DIM, q_t, 0.0).astype(BF16)
    m_scr[...] = jnp.full(m_scr.shape, NEG, F32)
    acc_scr[...] = jnp.zeros(acc_scr.shape, F32)

    all_tiles = tuple(range(0, 2 * tq, MXU_COLS))

    def softmax(s_in, c, key_offset):
        strip = slice(c, c + 128)
        s = s_in[:, strip]
        if key_offset is not None:
            kc = (lax.broadcasted_iota(jnp.int32, s.shape, 0) + key_offset) // CHUNK
            qc = (lax.broadcasted_iota(jnp.int32, s.shape, 1) + (c % tq)) // CHUNK
            s = jnp.where(kc <= qc, s, NEG)
        m_prev = m_scr[:, strip]
        m_new = jnp.maximum(m_prev, jnp.max(s, axis=0, keepdims=True))
        alpha = jnp.exp2(m_prev - m_new)
        p = jnp.exp2(s - m_new)
        m_scr[:, strip] = m_new
        return p.astype(BF16), alpha

    def phase(next_blk, s_next, cur_blk, s_cur, key_offset=None, next_tiles=all_tiles, cur_tiles=all_tiles):
        if next_blk is not None:
            k_blk = k_ref[pl.ds(pl.multiple_of(next_blk * tk, tk), tk), :]
        if cur_blk is not None:
            v_blk = vt_ref[cur_blk]
        for c in all_tiles:
            cols = slice(c, c + MXU_COLS)
            if next_blk is not None and c in next_tiles:
                s_next[:, cols] = jnp.dot(k_blk, qq_scr[:, cols], preferred_element_type=F32)
            if cur_blk is not None and c in cur_tiles:
                strips = [softmax(s_cur, cc, key_offset) for cc in range(c, c + MXU_COLS, 128)]
                p = jnp.concatenate([st[0] for st in strips], axis=1)
                alpha = jnp.concatenate([st[1] for st in strips], axis=1)
                pv = jnp.dot(v_blk, p, preferred_element_type=F32)
                acc_scr[:, cols] = alpha * acc_scr[:, cols] + pv

    bufs = (s_a, s_b)
    blocks_per_tile = tq // tk

    phase(0, s_a, None, None)

    def body(t, carry):
        for j in range(blocks_per_tile):
            blk = blocks_per_tile * t + j
            phase(blk + 1, bufs[(j + 1) % 2], blk, bufs[j % 2])
        return carry

    lax.fori_loop(0, qi, body, 0)

    for j in range(blocks_per_tile):
        blk = blocks_per_tile * qi + j
        seen_by = lambda j_: tuple(c for c in all_tiles if c % tq >= j_ * tk)
        if j + 1 < blocks_per_tile:
            phase(blk + 1, bufs[(j + 1) % 2], blk, bufs[j % 2], key_offset=j * tk,
                  next_tiles=seen_by(j + 1), cur_tiles=seen_by(j))
        else:
            phase(None, None, blk, bufs[j % 2], key_offset=j * tk, cur_tiles=seen_by(j))

    lam = _lambda(lq1_ref, lk1_ref, lq2_ref, lk2_ref, lam_init)
    accn = acc_scr[:V_DIM, :] * (1.0 / acc_scr[V_DIM:V_DIM + 1, :])
    o_t = accn[:, :tq] - lam * accn[:, tq:]
    o_ref[...] = _subln(o_t.T, sg_ref[...], lam_init).astype(BF16)


def _prompt_attn(lq1, lk1, lq2, lk2, subln_g, q, kb, vt, *, n_seq, seq_len, lam_init):
    t = n_seq * seq_len
    nq = seq_len // ATTN_TQ
    nk = seq_len // ATTN_TK
    vec = _resident((1, HEAD_DIM))
    stat = pltpu.VMEM((1, 2 * ATTN_TQ), F32)
    score = pltpu.VMEM((ATTN_TK, 2 * ATTN_TQ), F32)
    return pl.pallas_call(
        functools.partial(_prompt_attn_kernel, lam_init=lam_init),
        grid=(n_seq, N_HEADS, nq),
        in_specs=[vec, vec, vec, vec, _resident((1, V_DIM)),
                  pl.BlockSpec((ATTN_TQ, HEAD_COLS), lambda b, h, i: (b * nq + i, h)),
                  pl.BlockSpec((seq_len, HEAD_COLS), lambda b, h, i: (b, h)),
                  pl.BlockSpec((None, None, nk, VT_ROWS, ATTN_TK), lambda b, h, i: (b, h, 0, 0, 0))],
        out_specs=pl.BlockSpec((ATTN_TQ, V_DIM), lambda b, h, i: (b * nq + i, h)),
        out_shape=jax.ShapeDtypeStruct((t, N_HEADS * V_DIM), BF16),
        scratch_shapes=[pltpu.VMEM((HEAD_COLS, 2 * ATTN_TQ), BF16), score, score, stat,
                        pltpu.VMEM((VT_ROWS, 2 * ATTN_TQ), F32)],
        compiler_params=_params(3),
        name="prompt_attn",
    )(lq1, lk1, lq2, lk2, subln_g, q, kb, vt)


def _sample_attn_kernel(lq1_ref, lk1_ref, lq2_ref, lk2_ref, sg_ref, q_ref, kn_ref, vn_ref, ck_ref, cv_ref,
                        o_ref, *, lam_init):
    tq = q_ref.shape[0]
    past = ck_ref.shape[1]
    lam = _lambda(lq1_ref, lk1_ref, lq2_ref, lk2_ref, lam_init)
    nt = (((1,), (1,)), ((), ()))
    for hh in range(N_HEADS):
        cols = slice(hh * HEAD_COLS, (hh + 1) * HEAD_COLS)
        qq = _split_maps(q_ref[:, cols])
        s_c = jnp.dot(qq, ck_ref[cols, :].astype(BF16), preferred_element_type=F32)
        s_n = lax.dot_general(qq, kn_ref[:, cols].astype(BF16), nt, preferred_element_type=F32)
        m = jnp.maximum(jnp.max(s_c, axis=-1, keepdims=True), jnp.max(s_n, axis=-1, keepdims=True))
        p_c = jnp.exp2(s_c - m)
        p_n = jnp.exp2(s_n - m)
        inv = 1.0 / (jnp.sum(p_c, axis=-1, keepdims=True) + jnp.sum(p_n, axis=-1, keepdims=True))
        p_c = p_c * inv
        p_n = p_n * inv
        a_c = (p_c[:tq] - lam * p_c[tq:]).astype(BF16)
        a_n = (p_n[:tq] - lam * p_n[tq:]).astype(BF16)
        v_c = cv_ref[pl.ds(hh, past, stride=N_HEADS), :]
        o = (jnp.dot(a_c, v_c.astype(BF16), preferred_element_type=F32)
             + jnp.dot(a_n, vn_ref[:, cols].astype(BF16), preferred_element_type=F32))
        o_ref[:, cols] = _subln(o, sg_ref[...], lam_init).astype(BF16)


def _sample_attn(lq1, lk1, lq2, lk2, subln_g, q, k_new, v_new, cache_k, cache_v, *, n_seq, seq_len, lam_init):
    past = cache_k.shape[2]
    vec = _resident((1, HEAD_DIM))
    new = pl.BlockSpec((seq_len, D_MODEL), lambda b: (b, 0))
    old_k = pl.BlockSpec((None, D_MODEL, past), lambda b: (b, 0, 0))
    old_v = pl.BlockSpec((None, past * N_HEADS, V_DIM), lambda b: (b, 0, 0))
    return pl.pallas_call(
        functools.partial(_sample_attn_kernel, lam_init=lam_init),
        grid=(n_seq,),
        in_specs=[vec, vec, vec, vec, _resident((1, V_DIM)), new, new, new, old_k, old_v],
        out_specs=new,
        out_shape=jax.ShapeDtypeStruct((n_seq * seq_len, D_MODEL), BF16),
        compiler_params=_params(1),
        name="sample_attn",
    )(lq1, lk1, lq2, lk2, subln_g, q, k_new, v_new, cache_k, cache_v)


def _mix_kernel(o_ref, ga_ref, gy_ref, x_ref, wa_ref, wm_ref, gpost_ref, out_ref):
    y_attn = jnp.dot(o_ref[...], wa_ref[...], preferred_element_type=F32)
    merged = (ga_ref[...].astype(F32) * y_attn + gy_ref[...].astype(F32)).astype(BF16)
    y = jnp.dot(merged, wm_ref[...], preferred_element_type=F32)
    out_ref[...] = x_ref[...] + _rms(y, gpost_ref[...])


def _mix(o, gate_a, gated_conv, x, w_attn_out, w_mix_out, g_post):
    t = x.shape[0]
    row = pl.BlockSpec((ROW_TILE, D_MODEL), lambda i: (i, 0))
    return pl.pallas_call(
        _mix_kernel,
        grid=(t // ROW_TILE,),
        in_specs=[row, row, row, row, _resident((D_MODEL, D_MODEL)), _resident((D_MODEL, D_MODEL)),
                  _resident((1, D_MODEL))],
        out_specs=row,
        out_shape=jax.ShapeDtypeStruct((t, D_MODEL), F32),
        compiler_params=_params(1),
        name="mix",
    )(o, gate_a, gated_conv, x, w_attn_out, w_mix_out, g_post)


def kernel(x_prompt, x_sample, cache_k, cache_v, state_conv, ffn1_norm_pre, ffn1_norm_post, ffn1_w_gate, ffn1_w_up, ffn1_w_down, mix_norm_pre, mix_norm_post, w_in, b_gate, lambda_q1, lambda_k1, lambda_q2, lambda_k2, subln_g, w_attn_out, conv_w, conv_b, w_conv_out, w_mix_out, ffn2_norm_pre, ffn2_norm_post, ffn2_w_gate, ffn2_w_up, ffn2_w_down):
    depth = w_in.shape[0]
    n_p, s_p, _ = x_prompt.shape
    n_s, s_s, _ = x_sample.shape
    past = cache_k.shape[2]
    assert s_p % ROW_TILE == 0 and ROW_TILE % ATTN_TK == 0 and ATTN_TK % CHUNK == 0
    assert s_p % ATTN_TQ == 0 and ATTN_TQ % (2 * ATTN_TK) == 0
    assert ROW_TILE % s_s == 0 and (n_s * s_s) % ROW_TILE == 0 and s_s >= CONV_W - 1

    xp = x_prompt.reshape(n_p * s_p, D_MODEL)
    xs = x_sample.reshape(n_s * s_s, D_MODEL)
    outs = {name: [] for name in ("kp", "vp", "cp", "ks", "vs", "cs")}
    for l in range(depth):
        lam_init = _lambda_init(l)
        vec = lambda a: a[l][None, :]
        bf = lambda a: a[l].astype(BF16)
        ffn1 = (vec(ffn1_norm_pre), vec(ffn1_norm_post), bf(ffn1_w_gate), bf(ffn1_w_up), bf(ffn1_w_down))
        ffn2 = (vec(ffn2_norm_pre), vec(ffn2_norm_post), bf(ffn2_w_gate), bf(ffn2_w_up), bf(ffn2_w_down))
        proj = (vec(mix_norm_pre), bf(w_in), vec(b_gate), conv_w[l], vec(conv_b), bf(w_conv_out))
        lam = (vec(lambda_q1), vec(lambda_k1), vec(lambda_q2), vec(lambda_k2), vec(subln_g))
        mix_w = (bf(w_attn_out), bf(w_mix_out), vec(mix_norm_post))

        xp = _ffn(xp, *ffn1)
        zero_state = jnp.zeros((n_p, CONV_W - 1, D_MODEL), F32)
        q, k, v, kb, vt, ga, gy, conv = _inproj(xp, *proj, zero_state, seq_len=s_p, emit_kv=True)
        o = _prompt_attn(*lam, q, kb, vt, n_seq=n_p, seq_len=s_p, lam_init=lam_init)
        xp = _mix(o, ga, gy, xp, *mix_w)
        xp = _ffn(xp, *ffn2)
        outs["kp"].append(jnp.transpose(k.reshape(n_p, N_HEADS, 2, HEAD_DIM, s_p), (0, 4, 1, 2, 3)))
        outs["vp"].append(v.reshape(n_p, s_p, N_HEADS, V_DIM))
        outs["cp"].append(conv)

        xs = _ffn(xs, *ffn1)
        q, k, v, ga, gy, conv = _inproj(xs, *proj, state_conv[l], seq_len=s_s, emit_kv=False)
        cache_kt = jnp.transpose(cache_k[l], (0, 2, 3, 4, 1)).reshape(n_s, D_MODEL, past)
        cache_vr = cache_v[l].reshape(n_s, past * N_HEADS, V_DIM)
        o = _sample_attn(*lam, q, k, v, cache_kt, cache_vr, n_seq=n_s, seq_len=s_s, lam_init=lam_init)
        xs = _mix(o, ga, gy, xs, *mix_w)
        xs = _ffn(xs, *ffn2)
        outs["ks"].append(k.reshape(n_s, s_s, N_HEADS, 2, HEAD_DIM))
        outs["vs"].append(v.reshape(n_s, s_s, N_HEADS, V_DIM))
        outs["cs"].append(conv)

    return (xp.reshape(n_p, s_p, D_MODEL), xs.reshape(n_s, s_s, D_MODEL),
            jnp.stack(outs["kp"]), jnp.stack(outs["vp"]), jnp.stack(outs["cp"]),
            jnp.stack(outs["ks"]), jnp.stack(outs["vs"]), jnp.stack(outs["cs"]))
```

```python
import functools
import math

import jax
import jax.numpy as jnp
from jax import lax
from jax.experimental import pallas as pl
from jax.experimental.pallas import tpu as pltpu

D_MODEL = 1024
N_HEADS = 8
HEAD_DIM = 64
V_DIM = 2 * HEAD_DIM
HEAD_COLS = 2 * HEAD_DIM
D_FF = 2816
CHUNK = 64
CONV_W = 3
EPS = 1e-6
NEG = -1e30
LOG2E = 1.4426950408889634
Q_SCALE = (HEAD_DIM ** -0.5) * LOG2E

V7X_VMEM_LIMIT_BYTES = 58 * 1024 * 1024
ROW_TILE = 512
MXU_COLS = 256
FF_CHUNK = MXU_COLS
ATTN_TQ = 1024
ATTN_TK = 256
SCORE_LEAD = 4
BF16_SUBLANES = 16
VT_ROWS = V_DIM + BF16_SUBLANES

F32 = jnp.float32
BF16 = jnp.bfloat16


def _lambda_init(layer_idx):
    return 0.8 - 0.6 * math.exp(-0.3 * layer_idx)


def _rms(x, g):
    return x * lax.rsqrt(jnp.mean(x * x, axis=-1, keepdims=True) + EPS) * g


def _resident(shape):
    zeros = (0,) * len(shape)
    return pl.BlockSpec(shape, lambda *_: zeros, pipeline_mode=pl.Buffered(1))


def _params(n_axes):
    return pltpu.CompilerParams(
        dimension_semantics=("arbitrary",) * n_axes,
        vmem_limit_bytes=V7X_VMEM_LIMIT_BYTES,
    )


def _swiglu_residual(x, gpre_ref, gpost_ref, wg_ref, wu_ref, wd_ref, a_scr):
    h = _rms(x, gpre_ref[...]).astype(BF16)
    for c in range(0, D_FF, FF_CHUNK):
        g = jnp.dot(h, wg_ref[:, c:c + FF_CHUNK], preferred_element_type=F32)
        u = jnp.dot(h, wu_ref[:, c:c + FF_CHUNK], preferred_element_type=F32)
        a_scr[:, c:c + FF_CHUNK] = (g * jax.nn.sigmoid(g) * u).astype(BF16)
    y = jnp.dot(a_scr[...], wd_ref[...], preferred_element_type=F32)
    return x + 0.5 * _rms(y, gpost_ref[...])


def _ffn_kernel(x_ref, gpre_ref, gpost_ref, wg_ref, wu_ref, wd_ref, o_ref, a_scr):
    o_ref[...] = _swiglu_residual(x_ref[...], gpre_ref, gpost_ref, wg_ref, wu_ref, wd_ref, a_scr)


def _mix_ffn_kernel(o_ref, ga_ref, gy_ref, x_ref, wa_ref, wm_ref, gmix_ref,
                    gpre_ref, gpost_ref, wg_ref, wu_ref, wd_ref, out_ref, a_scr):
    y_attn = jnp.dot(o_ref[...], wa_ref[...], preferred_element_type=F32)
    merged = (ga_ref[...].astype(F32) * y_attn + gy_ref[...].astype(F32)).astype(BF16)
    y = jnp.dot(merged, wm_ref[...], preferred_element_type=F32)
    x = x_ref[...] + _rms(y, gmix_ref[...])
    out_ref[...] = _swiglu_residual(x, gpre_ref, gpost_ref, wg_ref, wu_ref, wd_ref, a_scr)


_FFN_WEIGHT_SPECS = (((1, D_MODEL)), ((1, D_MODEL)), (D_MODEL, D_FF), (D_MODEL, D_FF), (D_FF, D_MODEL))


def _ffn(x, g_pre, g_post, w_gate, w_up, w_down):
    t = x.shape[0]
    row = pl.BlockSpec((ROW_TILE, D_MODEL), lambda i: (i, 0))
    return pl.pallas_call(
        _ffn_kernel,
        grid=(t // ROW_TILE,),
        in_specs=[row] + [_resident(s) for s in _FFN_WEIGHT_SPECS],
        out_specs=row,
        out_shape=jax.ShapeDtypeStruct((t, D_MODEL), F32),
        scratch_shapes=[pltpu.VMEM((ROW_TILE, D_FF), BF16)],
        compiler_params=_params(1),
        name="ffn",
    )(x, g_pre, g_post, w_gate, w_up, w_down)


def _mix_ffn(o, gate_a, gated_conv, x, w_attn_out, w_mix_out, g_mix, g_pre, g_post, w_gate, w_up, w_down):
    t = x.shape[0]
    row = pl.BlockSpec((ROW_TILE, D_MODEL), lambda i: (i, 0))
    return pl.pallas_call(
        _mix_ffn_kernel,
        grid=(t // ROW_TILE,),
        in_specs=[row, row, row, row, _resident((D_MODEL, D_MODEL)), _resident((D_MODEL, D_MODEL)),
                  _resident((1, D_MODEL))] + [_resident(s) for s in _FFN_WEIGHT_SPECS],
        out_specs=row,
        out_shape=jax.ShapeDtypeStruct((t, D_MODEL), F32),
        scratch_shapes=[pltpu.VMEM((ROW_TILE, D_FF), BF16)],
        compiler_params=_params(1),
        name="mix_ffn",
    )(o, gate_a, gated_conv, x, w_attn_out, w_mix_out, g_mix, g_pre, g_post, w_gate, w_up, w_down)


def _inproj_kernel(x_ref, gpre_ref, w_ref, bg_ref, cw_ref, cb_ref, wco_ref, st_ref, *refs,
                   nb, lt, tiles_per_seq, emit_kv):
    if emit_kv:
        q_ref, k_ref, v_ref, kb_ref, vt_ref, ga_ref, gy_ref, nc_ref, carry_scr = refs
    else:
        q_ref, k_ref, v_ref, ga_ref, gy_ref, nc_ref, carry_scr = refs
    i = pl.program_id(0)
    h = _rms(x_ref[...], gpre_ref[...]).astype(BF16)

    def seg(j):
        return jnp.dot(h, w_ref[:, j * D_MODEL:(j + 1) * D_MODEL], preferred_element_type=F32)

    q_ref[...] = (seg(0) * Q_SCALE).astype(BF16)
    k = seg(1)
    k_ref[...] = k.T if emit_kv else k
    v = seg(2)
    v_ref[...] = v
    if emit_kv:
        kb_ref[...] = k.astype(BF16)
        for hh in range(N_HEADS):
            for kb in range(ROW_TILE // ATTN_TK):
                vt_ref[hh, kb, :V_DIM, :] = (
                    v[kb * ATTN_TK:(kb + 1) * ATTN_TK, hh * V_DIM:(hh + 1) * V_DIM].T.astype(BF16))
                vt_ref[hh, kb, V_DIM:, :] = jnp.ones((BF16_SUBLANES, ATTN_TK), BF16)

    @pl.when(i % tiles_per_seq == 0)
    def _():
        carry_scr[...] = st_ref[...]

    gate_b = seg(3).reshape(nb, lt, D_MODEL)
    u = (seg(4) * seg(5)).reshape(nb, lt, D_MODEL)
    prev = carry_scr[...]
    prev0 = prev[:, 0:1, :]
    prev1 = prev[:, 1:2, :]
    pos = lax.broadcasted_iota(jnp.int32, u.shape, 1)
    back1 = jnp.where(pos == 0, prev1, pltpu.roll(u, 1, 1))
    back2 = jnp.where(pos == 0, prev0, jnp.where(pos == 1, prev1, pltpu.roll(u, 2, 1)))
    cw = cw_ref[...]
    zc = cb_ref[...] + cw[0:1, :] * back2 + cw[1:2, :] * back1 + cw[2:3, :] * u
    tail = u[:, lt - (CONV_W - 1):, :]
    carry_scr[...] = tail
    nc_ref[...] = tail
    c_in = (gate_b * zc).reshape(nb * lt, D_MODEL).astype(BF16)
    y_conv = jnp.dot(c_in, wco_ref[...], preferred_element_type=F32)

    bg = bg_ref[...]
    ga_ref[...] = jax.nn.sigmoid(seg(6) + bg[:, :D_MODEL]).astype(BF16)
    gy_ref[...] = (jax.nn.sigmoid(seg(7) + bg[:, D_MODEL:]) * y_conv).astype(BF16)


def _inproj(x, g_pre, w_in, b_gate, conv_w, conv_b, w_conv_out, state, *, seq_len, emit_kv):
    t = x.shape[0]
    n_seq = t // seq_len
    lt = min(seq_len, ROW_TILE)
    nb = ROW_TILE // lt
    tiles_per_seq = seq_len // lt
    n_tiles = t // ROW_TILE
    row = lambda dt: (pl.BlockSpec((ROW_TILE, D_MODEL), lambda i: (i, 0)),
                      jax.ShapeDtypeStruct((t, D_MODEL), dt))
    seq_block = pl.BlockSpec((nb, CONV_W - 1, D_MODEL), lambda i: (i // tiles_per_seq, 0, 0))
    outs = [row(BF16), row(F32), row(F32)]
    if emit_kv:
        outs[1] = (pl.BlockSpec((None, D_MODEL, ROW_TILE), lambda i: (i // tiles_per_seq, 0, i % tiles_per_seq)),
                   jax.ShapeDtypeStruct((n_seq, D_MODEL, seq_len), F32))
        outs.append(row(BF16))
        kb_per_tile = ROW_TILE // ATTN_TK
        outs.append((pl.BlockSpec((None, N_HEADS, kb_per_tile, VT_ROWS, ATTN_TK),
                                  lambda i: (i // tiles_per_seq, 0, i % tiles_per_seq, 0, 0)),
                     jax.ShapeDtypeStruct((n_seq, N_HEADS, tiles_per_seq * kb_per_tile, VT_ROWS, ATTN_TK), BF16)))
    outs += [row(BF16), row(BF16),
             (seq_block, jax.ShapeDtypeStruct((n_seq, CONV_W - 1, D_MODEL), F32))]
    kern = functools.partial(_inproj_kernel, nb=nb, lt=lt, tiles_per_seq=tiles_per_seq, emit_kv=emit_kv)
    return pl.pallas_call(
        kern,
        grid=(n_tiles,),
        in_specs=[pl.BlockSpec((ROW_TILE, D_MODEL), lambda i: (i, 0)),
                  _resident((1, D_MODEL)), _resident((D_MODEL, 8 * D_MODEL)), _resident((1, 2 * D_MODEL)),
                  _resident((CONV_W, D_MODEL)), _resident((1, D_MODEL)), _resident((D_MODEL, D_MODEL)),
                  seq_block],
        out_specs=[o[0] for o in outs],
        out_shape=[o[1] for o in outs],
        scratch_shapes=[pltpu.VMEM((nb, CONV_W - 1, D_MODEL), F32)],
        compiler_params=_params(1),
        name="inproj",
    )(x, g_pre, w_in, b_gate, conv_w, conv_b, w_conv_out, state)


def _lambda(lq1_ref, lk1_ref, lq2_ref, lk2_ref, lam_init):
    a = jnp.sum(lq1_ref[...] * lk1_ref[...], axis=-1, keepdims=True)
    b = jnp.sum(lq2_ref[...] * lk2_ref[...], axis=-1, keepdims=True)
    return jnp.exp(a) - jnp.exp(b) + lam_init


def _split_maps(qb):
    lane = lax.broadcasted_iota(jnp.int32, qb.shape, 1)
    zero = jnp.zeros_like(qb)
    return jnp.concatenate([jnp.where(lane < HEAD_DIM, qb, zero),
                            jnp.where(lane >= HEAD_DIM, qb, zero)], axis=0)


def _subln(o, g, lam_init):
    return (o * lax.rsqrt(jnp.mean(o * o, axis=-1, keepdims=True) + EPS) * g) * (1.0 - lam_init)


def _prompt_attn_kernel(lq1_ref, lk1_ref, lq2_ref, lk2_ref, sg_ref, q_ref, k_ref, vt_ref, o_ref,
                        qq_scr, s_a, s_b, m_scr, acc_scr, *, lam_init, n_q):
    tq, tk = ATTN_TQ, ATTN_TK
    bufs = (s_a, s_b)
    n_ct = 2 * tq // MXU_COLS
    ct_per_map = tq // MXU_COLS
    blocks_per_tile = tq // tk
    lam = _lambda(lq1_ref, lk1_ref, lq2_ref, lk2_ref, lam_init)

    def build_qq(qi):
        q_t = q_ref[pl.ds(pl.multiple_of(qi * tq, tq), tq), :].astype(F32).T
        row = lax.broadcasted_iota(jnp.int32, q_t.shape, 0)
        qq_scr[:, :tq] = jnp.where(row < HEAD_DIM, q_t, 0.0).astype(BF16)
        qq_scr[:, tq:] = jnp.where(row >= HEAD_DIM, q_t, 0.0).astype(BF16)

    def init_state():
        m_scr[...] = jnp.full(m_scr.shape, NEG, F32)
        acc_scr[...] = jnp.zeros(acc_scr.shape, F32)

    def scores(blk, parity, ct):
        cols = slice(ct * MXU_COLS, (ct + 1) * MXU_COLS)
        k_blk = k_ref[pl.ds(pl.multiple_of(blk * tk, tk), tk), :]
        bufs[parity][:, cols] = jnp.dot(k_blk, qq_scr[:, cols], preferred_element_type=F32)

    def softmax(s_in, c, key_offset):
        strip = slice(c, c + 128)
        s = s_in[:, strip]
        if key_offset is not None:
            kc = (lax.broadcasted_iota(jnp.int32, s.shape, 0) + key_offset) // CHUNK
            qc = (lax.broadcasted_iota(jnp.int32, s.shape, 1) + (c % tq)) // CHUNK
            s = jnp.where(kc <= qc, s, NEG)
        m_prev = m_scr[:, strip]
        m_new = jnp.maximum(m_prev, jnp.max(s, axis=0, keepdims=True))
        alpha = jnp.exp2(m_prev - m_new)
        p = jnp.exp2(s - m_new)
        m_scr[:, strip] = m_new
        return p.astype(BF16), alpha

    def attend(blk, parity, ct, key_offset=None):
        c = ct * MXU_COLS
        cols = slice(c, c + MXU_COLS)
        strips = [softmax(bufs[parity], cc, key_offset) for cc in range(c, c + MXU_COLS, 128)]
        p = jnp.concatenate([st[0] for st in strips], axis=1)
        alpha = jnp.concatenate([st[1] for st in strips], axis=1)
        pv = jnp.dot(vt_ref[blk], p, preferred_element_type=F32)
        acc_scr[:, cols] = alpha * acc_scr[:, cols] + pv

    def finalize(qi):
        accn = acc_scr[:V_DIM, :] * (1.0 / acc_scr[V_DIM:V_DIM + 1, :])
        o_t = accn[:, :tq] - lam * accn[:, tq:]
        o_ref[pl.ds(pl.multiple_of(qi * tq, tq), tq), :] = _subln(o_t.T, sg_ref[...], lam_init).astype(BF16)

    past_units = [(j, ct) for j in range(blocks_per_tile) for ct in range(n_ct)]

    def past_body(t, carry):
        base = blocks_per_tile * t
        for n, (j, ct) in enumerate(past_units):
            ahead = n + SCORE_LEAD
            j2, ct2 = past_units[ahead % len(past_units)]
            if ahead >= len(past_units):
                j2 += blocks_per_tile
            scores(base + j2, j2 % 2, ct2)
            attend(base + j, j % 2, ct)
        return carry

    own_units = [(j, ct) for j in range(blocks_per_tile) for ct in range(n_ct)
                 if (ct % ct_per_map) * MXU_COLS >= j * tk]

    def tile_body(qi, carry):
        lax.fori_loop(0, qi, past_body, 0)
        base = blocks_per_tile * qi
        next_qi = jnp.minimum(qi + 1, n_q - 1)
        for n, (j, ct) in enumerate(own_units):
            ahead = n + SCORE_LEAD
            if ahead < len(own_units):
                j2, ct2 = own_units[ahead]
                scores(base + j2, j2 % 2, ct2)
            else:
                if ahead == len(own_units):
                    build_qq(next_qi)
                scores(0, 0, ahead - len(own_units))
            attend(base + j, j % 2, ct, key_offset=j * tk)
        finalize(qi)
        init_state()
        return carry

    build_qq(0)
    init_state()
    for ct in range(SCORE_LEAD):
        scores(0, 0, ct)
    lax.fori_loop(0, n_q, tile_body, 0)


def _prompt_attn(lq1, lk1, lq2, lk2, subln_g, q, kb, vt, *, n_seq, seq_len, lam_init):
    t = n_seq * seq_len
    nq = seq_len // ATTN_TQ
    nk = seq_len // ATTN_TK
    vec = _resident((1, HEAD_DIM))
    stat = pltpu.VMEM((1, 2 * ATTN_TQ), F32)
    score = pltpu.VMEM((ATTN_TK, 2 * ATTN_TQ), F32)
    head_rows = pl.BlockSpec((seq_len, HEAD_COLS), lambda b, h: (b, h))
    return pl.pallas_call(
        functools.partial(_prompt_attn_kernel, lam_init=lam_init, n_q=nq),
        grid=(n_seq, N_HEADS),
        in_specs=[vec, vec, vec, vec, _resident((1, V_DIM)), head_rows, head_rows,
                  pl.BlockSpec((None, None, nk, VT_ROWS, ATTN_TK), lambda b, h: (b, h, 0, 0, 0))],
        out_specs=head_rows,
        out_shape=jax.ShapeDtypeStruct((t, N_HEADS * V_DIM), BF16),
        scratch_shapes=[pltpu.VMEM((HEAD_COLS, 2 * ATTN_TQ), BF16), score, score, stat,
                        pltpu.VMEM((VT_ROWS, 2 * ATTN_TQ), F32)],
        compiler_params=_params(2),
        name="prompt_attn",
    )(lq1, lk1, lq2, lk2, subln_g, q, kb, vt)


def _sample_attn_kernel(lq1_ref, lk1_ref, lq2_ref, lk2_ref, sg_ref, q_ref, kn_ref, vn_ref, ck_ref, cv_ref,
                        o_ref, *, lam_init):
    tq = q_ref.shape[0]
    past = ck_ref.shape[1]
    lam = _lambda(lq1_ref, lk1_ref, lq2_ref, lk2_ref, lam_init)
    nt = (((1,), (1,)), ((), ()))
    for hh in range(N_HEADS):
        cols = slice(hh * HEAD_COLS, (hh + 1) * HEAD_COLS)
        qq = _split_maps(q_ref[:, cols])
        s_c = jnp.dot(qq, ck_ref[cols, :].astype(BF16), preferred_element_type=F32)
        s_n = lax.dot_general(qq, kn_ref[:, cols].astype(BF16), nt, preferred_element_type=F32)
        m = jnp.maximum(jnp.max(s_c, axis=-1, keepdims=True), jnp.max(s_n, axis=-1, keepdims=True))
        p_c = jnp.exp2(s_c - m)
        p_n = jnp.exp2(s_n - m)
        inv = 1.0 / (jnp.sum(p_c, axis=-1, keepdims=True) + jnp.sum(p_n, axis=-1, keepdims=True))
        p_c = p_c * inv
        p_n = p_n * inv
        a_c = (p_c[:tq] - lam * p_c[tq:]).astype(BF16)
        a_n = (p_n[:tq] - lam * p_n[tq:]).astype(BF16)
        v_c = cv_ref[pl.ds(hh, past, stride=N_HEADS), :]
        o = (jnp.dot(a_c, v_c.astype(BF16), preferred_element_type=F32)
             + jnp.dot(a_n, vn_ref[:, cols].astype(BF16), preferred_element_type=F32))
        o_ref[:, cols] = _subln(o, sg_ref[...], lam_init).astype(BF16)


def _sample_attn(lq1, lk1, lq2, lk2, subln_g, q, k_new, v_new, cache_k, cache_v, *, n_seq, seq_len, lam_init):
    past = cache_k.shape[2]
    vec = _resident((1, HEAD_DIM))
    new = pl.BlockSpec((seq_len, D_MODEL), lambda b: (b, 0))
    old_k = pl.BlockSpec((None, D_MODEL, past), lambda b: (b, 0, 0))
    old_v = pl.BlockSpec((None, past * N_HEADS, V_DIM), lambda b: (b, 0, 0))
    return pl.pallas_call(
        functools.partial(_sample_attn_kernel, lam_init=lam_init),
        grid=(n_seq,),
        in_specs=[vec, vec, vec, vec, _resident((1, V_DIM)), new, new, new, old_k, old_v],
        out_specs=new,
        out_shape=jax.ShapeDtypeStruct((n_seq * seq_len, D_MODEL), BF16),
        compiler_params=_params(1),
        name="sample_attn",
    )(lq1, lk1, lq2, lk2, subln_g, q, k_new, v_new, cache_k, cache_v)


def kernel(x_prompt, x_sample, cache_k, cache_v, state_conv, ffn1_norm_pre, ffn1_norm_post, ffn1_w_gate, ffn1_w_up, ffn1_w_down, mix_norm_pre, mix_norm_post, w_in, b_gate, lambda_q1, lambda_k1, lambda_q2, lambda_k2, subln_g, w_attn_out, conv_w, conv_b, w_conv_out, w_mix_out, ffn2_norm_pre, ffn2_norm_post, ffn2_w_gate, ffn2_w_up, ffn2_w_down):
    depth = w_in.shape[0]
    n_p, s_p, _ = x_prompt.shape
    n_s, s_s, _ = x_sample.shape
    past = cache_k.shape[2]
    assert s_p % ROW_TILE == 0 and ROW_TILE % ATTN_TK == 0 and ATTN_TK % CHUNK == 0
    assert s_p % ATTN_TQ == 0 and ATTN_TQ % (2 * ATTN_TK) == 0 and SCORE_LEAD <= ATTN_TQ // MXU_COLS
    assert ROW_TILE % s_s == 0 and (n_s * s_s) % ROW_TILE == 0 and s_s >= CONV_W - 1

    xp = x_prompt.reshape(n_p * s_p, D_MODEL)
    xs = x_sample.reshape(n_s * s_s, D_MODEL)
    outs = {name: [] for name in ("kp", "vp", "cp", "ks", "vs", "cs")}
    for l in range(depth):
        lam_init = _lambda_init(l)
        vec = lambda a: a[l][None, :]
        bf = lambda a: a[l].astype(BF16)
        ffn1 = (vec(ffn1_norm_pre), vec(ffn1_norm_post), bf(ffn1_w_gate), bf(ffn1_w_up), bf(ffn1_w_down))
        ffn2 = (vec(ffn2_norm_pre), vec(ffn2_norm_post), bf(ffn2_w_gate), bf(ffn2_w_up), bf(ffn2_w_down))
        proj = (vec(mix_norm_pre), bf(w_in), vec(b_gate), conv_w[l], vec(conv_b), bf(w_conv_out))
        lam = (vec(lambda_q1), vec(lambda_k1), vec(lambda_q2), vec(lambda_k2), vec(subln_g))
        mix_w = (bf(w_attn_out), bf(w_mix_out), vec(mix_norm_post))

        xp = _ffn(xp, *ffn1)
        zero_state = jnp.zeros((n_p, CONV_W - 1, D_MODEL), F32)
        q, k, v, kb, vt, ga, gy, conv = _inproj(xp, *proj, zero_state, seq_len=s_p, emit_kv=True)
        o = _prompt_attn(*lam, q, kb, vt, n_seq=n_p, seq_len=s_p, lam_init=lam_init)
        xp = _mix_ffn(o, ga, gy, xp, *mix_w, *ffn2)
        outs["kp"].append(jnp.transpose(k.reshape(n_p, N_HEADS, 2, HEAD_DIM, s_p), (0, 4, 1, 2, 3)))
        outs["vp"].append(v.reshape(n_p, s_p, N_HEADS, V_DIM))
        outs["cp"].append(conv)

        xs = _ffn(xs, *ffn1)
        q, k, v, ga, gy, conv = _inproj(xs, *proj, state_conv[l], seq_len=s_s, emit_kv=False)
        cache_kt = jnp.transpose(cache_k[l], (0, 2, 3, 4, 1)).reshape(n_s, D_MODEL, past)
        cache_vr = cache_v[l].reshape(n_s, past * N_HEADS, V_DIM)
        o = _sample_attn(*lam, q, k, v, cache_kt, cache_vr, n_seq=n_s, seq_len=s_s, lam_init=lam_init)
        xs = _mix_ffn(o, ga, gy, xs, *mix_w, *ffn2)
        outs["ks"].append(k.reshape(n_s, s_s, N_HEADS, 2, HEAD_DIM))
        outs["vs"].append(v.reshape(n_s, s_s, N_HEADS, V_DIM))
        outs["cs"].append(conv)

    return (xp.reshape(n_p, s_p, D_MODEL), xs.reshape(n_s, s_s, D_MODEL),
            jnp.stack(outs["kp"]), jnp.stack(outs["vp"]), jnp.stack(outs["cp"]),
            jnp.stack(outs["ks"]), jnp.stack(outs["vs"]), jnp.stack(outs["cs"]))
```

```python
import functools
import math

import jax
import jax.numpy as jnp
from jax import lax
from jax.experimental import pallas as pl
from jax.experimental.pallas import tpu as pltpu

D_MODEL = 1024
N_HEADS = 8
HEAD_DIM = 64
V_DIM = 2 * HEAD_DIM
HEAD_COLS = 2 * HEAD_DIM
D_FF = 2816
CHUNK = 64
CONV_W = 3
EPS = 1e-6
NEG = -1e30
LOG2E = 1.4426950408889634
Q_SCALE = (HEAD_DIM ** -0.5) * LOG2E

V7X_VMEM_LIMIT_BYTES = 58 * 1024 * 1024
ROW_TILE = 512
MXU_COLS = 256
FF_CHUNK = MXU_COLS
ATTN_TQ = 2048
ATTN_TK = 256
SCORE_LEAD = 4
BF16_SUBLANES = 16
VT_ROWS = V_DIM + BF16_SUBLANES

F32 = jnp.float32
BF16 = jnp.bfloat16


def _lambda_init(layer_idx):
    return 0.8 - 0.6 * math.exp(-0.3 * layer_idx)


def _rms(x, g):
    return x * lax.rsqrt(jnp.mean(x * x, axis=-1, keepdims=True) + EPS) * g


def _resident(shape):
    zeros = (0,) * len(shape)
    return pl.BlockSpec(shape, lambda *_: zeros, pipeline_mode=pl.Buffered(1))


def _params(n_axes):
    return pltpu.CompilerParams(
        dimension_semantics=("arbitrary",) * n_axes,
        vmem_limit_bytes=V7X_VMEM_LIMIT_BYTES,
    )


def _swiglu_residual(x, gpre_ref, gpost_ref, wg_ref, wu_ref, wd_ref, a_scr):
    h = _rms(x, gpre_ref[...]).astype(BF16)
    for c in range(0, D_FF, FF_CHUNK):
        cols = slice(c, min(c + FF_CHUNK, D_FF))
        g = jnp.dot(h, wg_ref[:, cols], preferred_element_type=F32)
        u = jnp.dot(h, wu_ref[:, cols], preferred_element_type=F32)
        a_scr[:, cols] = (g * jax.nn.sigmoid(g) * u).astype(BF16)
    y = jnp.dot(a_scr[...], wd_ref[...], preferred_element_type=F32)
    return x + 0.5 * _rms(y, gpost_ref[...])


def _ffn_kernel(x_ref, gpre_ref, gpost_ref, wg_ref, wu_ref, wd_ref, o_ref, a_scr):
    o_ref[...] = _swiglu_residual(x_ref[...], gpre_ref, gpost_ref, wg_ref, wu_ref, wd_ref, a_scr)


def _mix_ffn_kernel(o_ref, ga_ref, gy_ref, x_ref, wa_ref, wm_ref, gmix_ref,
                    gpre_ref, gpost_ref, wg_ref, wu_ref, wd_ref, out_ref, a_scr):
    y_attn = jnp.dot(o_ref[...], wa_ref[...], preferred_element_type=F32)
    merged = (ga_ref[...].astype(F32) * y_attn + gy_ref[...].astype(F32)).astype(BF16)
    y = jnp.dot(merged, wm_ref[...], preferred_element_type=F32)
    x = x_ref[...] + _rms(y, gmix_ref[...])
    out_ref[...] = _swiglu_residual(x, gpre_ref, gpost_ref, wg_ref, wu_ref, wd_ref, a_scr)


_FFN_WEIGHT_SPECS = (((1, D_MODEL)), ((1, D_MODEL)), (D_MODEL, D_FF), (D_MODEL, D_FF), (D_FF, D_MODEL))


def _ffn(x, g_pre, g_post, w_gate, w_up, w_down):
    t = x.shape[0]
    row = pl.BlockSpec((ROW_TILE, D_MODEL), lambda i: (i, 0))
    return pl.pallas_call(
        _ffn_kernel,
        grid=(t // ROW_TILE,),
        in_specs=[row] + [_resident(s) for s in _FFN_WEIGHT_SPECS],
        out_specs=row,
        out_shape=jax.ShapeDtypeStruct((t, D_MODEL), F32),
        scratch_shapes=[pltpu.VMEM((ROW_TILE, D_FF), BF16)],
        compiler_params=_params(1),
        name="ffn",
    )(x, g_pre, g_post, w_gate, w_up, w_down)


def _mix_ffn(o, gate_a, gated_conv, x, w_attn_out, w_mix_out, g_mix, g_pre, g_post, w_gate, w_up, w_down):
    t = x.shape[0]
    row = pl.BlockSpec((ROW_TILE, D_MODEL), lambda i: (i, 0))
    return pl.pallas_call(
        _mix_ffn_kernel,
        grid=(t // ROW_TILE,),
        in_specs=[row, row, row, row, _resident((D_MODEL, D_MODEL)), _resident((D_MODEL, D_MODEL)),
                  _resident((1, D_MODEL))] + [_resident(s) for s in _FFN_WEIGHT_SPECS],
        out_specs=row,
        out_shape=jax.ShapeDtypeStruct((t, D_MODEL), F32),
        scratch_shapes=[pltpu.VMEM((ROW_TILE, D_FF), BF16)],
        compiler_params=_params(1),
        name="mix_ffn",
    )(o, gate_a, gated_conv, x, w_attn_out, w_mix_out, g_mix, g_pre, g_post, w_gate, w_up, w_down)


def _inproj_kernel(x_ref, gpre_ref, w_ref, bg_ref, cw_ref, cb_ref, wco_ref, st_ref, *refs,
                   nb, lt, tiles_per_seq, emit_kv):
    if emit_kv:
        q_ref, k_ref, v_ref, kb_ref, vt_ref, ga_ref, gy_ref, nc_ref, carry_scr = refs
    else:
        q_ref, k_ref, v_ref, ga_ref, gy_ref, nc_ref, carry_scr = refs
    i = pl.program_id(0)
    h = _rms(x_ref[...], gpre_ref[...]).astype(BF16)

    def seg(j):
        return jnp.dot(h, w_ref[:, j * D_MODEL:(j + 1) * D_MODEL], preferred_element_type=F32)

    q = seg(0) * Q_SCALE
    if emit_kv:
        for hh in range(N_HEADS):
            q_ref[hh] = q[:, hh * HEAD_COLS:(hh + 1) * HEAD_COLS].T.astype(BF16)
    else:
        q_ref[...] = q.astype(BF16)
    k = seg(1)
    k_ref[...] = k.T if emit_kv else k
    v = seg(2)
    v_ref[...] = v
    if emit_kv:
        kb_ref[...] = k.astype(BF16)
        for hh in range(N_HEADS):
            for kb in range(ROW_TILE // ATTN_TK):
                vt_ref[hh, kb, :V_DIM, :] = (
                    v[kb * ATTN_TK:(kb + 1) * ATTN_TK, hh * V_DIM:(hh + 1) * V_DIM].T.astype(BF16))
                vt_ref[hh, kb, V_DIM:, :] = jnp.ones((BF16_SUBLANES, ATTN_TK), BF16)

    @pl.when(i % tiles_per_seq == 0)
    def _():
        carry_scr[...] = st_ref[...]

    gate_b = seg(3).reshape(nb, lt, D_MODEL)
    u = (seg(4) * seg(5)).reshape(nb, lt, D_MODEL)
    prev = carry_scr[...]
    prev0 = prev[:, 0:1, :]
    prev1 = prev[:, 1:2, :]
    pos = lax.broadcasted_iota(jnp.int32, u.shape, 1)
    back1 = jnp.where(pos == 0, prev1, pltpu.roll(u, 1, 1))
    back2 = jnp.where(pos == 0, prev0, jnp.where(pos == 1, prev1, pltpu.roll(u, 2, 1)))
    cw = cw_ref[...]
    zc = cb_ref[...] + cw[0:1, :] * back2 + cw[1:2, :] * back1 + cw[2:3, :] * u
    tail = u[:, lt - (CONV_W - 1):, :]
    carry_scr[...] = tail
    nc_ref[...] = tail
    c_in = (gate_b * zc).reshape(nb * lt, D_MODEL).astype(BF16)
    y_conv = jnp.dot(c_in, wco_ref[...], preferred_element_type=F32)

    bg = bg_ref[...]
    ga_ref[...] = jax.nn.sigmoid(seg(6) + bg[:, :D_MODEL]).astype(BF16)
    gy_ref[...] = (jax.nn.sigmoid(seg(7) + bg[:, D_MODEL:]) * y_conv).astype(BF16)


def _inproj(x, g_pre, w_in, b_gate, conv_w, conv_b, w_conv_out, state, *, seq_len, emit_kv):
    t = x.shape[0]
    n_seq = t // seq_len
    lt = min(seq_len, ROW_TILE)
    nb = ROW_TILE // lt
    tiles_per_seq = seq_len // lt
    n_tiles = t // ROW_TILE
    row = lambda dt: (pl.BlockSpec((ROW_TILE, D_MODEL), lambda i: (i, 0)),
                      jax.ShapeDtypeStruct((t, D_MODEL), dt))
    seq_block = pl.BlockSpec((nb, CONV_W - 1, D_MODEL), lambda i: (i // tiles_per_seq, 0, 0))
    outs = [row(BF16), row(F32), row(F32)]
    if emit_kv:
        outs[0] = (pl.BlockSpec((None, N_HEADS, None, HEAD_COLS, ROW_TILE),
                                lambda i: (i // tiles_per_seq, 0, i % tiles_per_seq, 0, 0)),
                   jax.ShapeDtypeStruct((n_seq, N_HEADS, tiles_per_seq, HEAD_COLS, ROW_TILE), BF16))
        outs[1] =(pl.BlockSpec((None, D_MODEL, ROW_TILE), lambda i: (i // tiles_per_seq, 0, i % tiles_per_seq)),
                   jax.ShapeDtypeStruct((n_seq, D_MODEL, seq_len), F32))
        outs.append(row(BF16))
        kb_per_tile = ROW_TILE // ATTN_TK
        outs.append((pl.BlockSpec((None, N_HEADS, kb_per_tile, VT_ROWS, ATTN_TK),
                                  lambda i: (i // tiles_per_seq, 0, i % tiles_per_seq, 0, 0)),
                     jax.ShapeDtypeStruct((n_seq, N_HEADS, tiles_per_seq * kb_per_tile, VT_ROWS, ATTN_TK), BF16)))
    outs += [row(BF16), row(BF16),
             (seq_block, jax.ShapeDtypeStruct((n_seq, CONV_W - 1, D_MODEL), F32))]
    kern = functools.partial(_inproj_kernel, nb=nb, lt=lt, tiles_per_seq=tiles_per_seq, emit_kv=emit_kv)
    return pl.pallas_call(
        kern,
        grid=(n_tiles,),
        in_specs=[pl.BlockSpec((ROW_TILE, D_MODEL), lambda i: (i, 0)),
                  _resident((1, D_MODEL)), _resident((D_MODEL, 8 * D_MODEL)), _resident((1, 2 * D_MODEL)),
                  _resident((CONV_W, D_MODEL)), _resident((1, D_MODEL)), _resident((D_MODEL, D_MODEL)),
                  seq_block],
        out_specs=[o[0] for o in outs],
        out_shape=[o[1] for o in outs],
        scratch_shapes=[pltpu.VMEM((nb, CONV_W - 1, D_MODEL), F32)],
        compiler_params=_params(1),
        name="inproj",
    )(x, g_pre, w_in, b_gate, conv_w, conv_b, w_conv_out, state)


def _lambda(lq1_ref, lk1_ref, lq2_ref, lk2_ref, lam_init):
    a = jnp.sum(lq1_ref[...] * lk1_ref[...], axis=-1, keepdims=True)
    b = jnp.sum(lq2_ref[...] * lk2_ref[...], axis=-1, keepdims=True)
    return jnp.exp(a) - jnp.exp(b) + lam_init


def _split_maps(qb):
    lane = lax.broadcasted_iota(jnp.int32, qb.shape, 1)
    zero = jnp.zeros_like(qb)
    return jnp.concatenate([jnp.where(lane < HEAD_DIM, qb, zero),
                            jnp.where(lane >= HEAD_DIM, qb, zero)], axis=0)


def _subln(o, g, lam_init):
    return (o * lax.rsqrt(jnp.mean(o * o, axis=-1, keepdims=True) + EPS) * g) * (1.0 - lam_init)


def _prompt_attn_kernel(lq1_ref, lk1_ref, lq2_ref, lk2_ref, sg_ref, q_ref, k_ref, vt_ref, o_ref,
                        qq_scr, s_a, s_b, acc_scr, m_scr, *, lam_init, n_q):
    tq, tk = ATTN_TQ, ATTN_TK
    bufs = (s_a, s_b)
    n_ct = 2 * tq // MXU_COLS
    ct_per_map = tq // MXU_COLS
    blocks_per_tile = tq // tk
    lam = _lambda(lq1_ref, lk1_ref, lq2_ref, lk2_ref, lam_init)

    def build_qq(qi):
        parts = tq // ROW_TILE
        for part in range(parts):
            q_t = q_ref[parts * qi + part]
            row = lax.broadcasted_iota(jnp.int32, q_t.shape, 0)
            zero = jnp.zeros_like(q_t)
            lo = part * ROW_TILE
            qq_scr[:, lo:lo + ROW_TILE] = jnp.where(row < HEAD_DIM, q_t, zero)
            qq_scr[:, tq + lo:tq + lo + ROW_TILE] = jnp.where(row >= HEAD_DIM, q_t, zero)

    def init_state():
        m_scr[...] = jnp.full(m_scr.shape, NEG, F32)
        acc_scr[...] = jnp.zeros(acc_scr.shape, F32)

    def scores(blk, parity, ct):
        cols = slice(ct * MXU_COLS, (ct + 1) * MXU_COLS)
        k_blk = k_ref[pl.ds(pl.multiple_of(blk * tk, tk), tk), :]
        bufs[parity][:, cols] = jnp.dot(k_blk, qq_scr[:, cols], preferred_element_type=F32)

    def softmax(s_in, c, key_offset):
        strip = slice(c, c + 128)
        s = s_in[:, strip]
        if key_offset is not None:
            kc = (lax.broadcasted_iota(jnp.int32, s.shape, 0) + key_offset) // CHUNK
            qc = (lax.broadcasted_iota(jnp.int32, s.shape, 1) + (c % tq)) // CHUNK
            s = jnp.where(kc <= qc, s, NEG)
        m_prev = m_scr[:, strip]
        m_new = jnp.maximum(m_prev, jnp.max(s, axis=0, keepdims=True))
        alpha = jnp.exp2(m_prev - m_new)
        p = jnp.exp2(s - m_new)
        m_scr[:, strip] = m_new
        return p.astype(BF16), alpha

    def attend(blk, parity, ct, key_offset=None):
        c = ct * MXU_COLS
        cols = slice(c, c + MXU_COLS)
        strips = [softmax(bufs[parity], cc, key_offset) for cc in range(c, c + MXU_COLS, 128)]
        p = jnp.concatenate([st[0] for st in strips], axis=1)
        alpha = jnp.concatenate([st[1] for st in strips], axis=1)
        pv = jnp.dot(vt_ref[blk], p, preferred_element_type=F32)
        acc_scr[:, cols] = alpha * acc_scr[:, cols] + pv

    def finalize(qi, part):
        lo = part * MXU_COLS
        o1 = acc_scr[:V_DIM, lo:lo + MXU_COLS] * (1.0 / acc_scr[V_DIM:V_DIM + 1, lo:lo + MXU_COLS])
        o2 = (acc_scr[:V_DIM, tq + lo:tq + lo + MXU_COLS]
              * (1.0 / acc_scr[V_DIM:V_DIM + 1, tq + lo:tq + lo + MXU_COLS]))
        o_t = o1 - lam * o2
        rows = pl.ds(pl.multiple_of(qi * tq + lo, MXU_COLS), MXU_COLS)
        o_ref[rows, :] = _subln(o_t.T, sg_ref[...], lam_init).astype(BF16)

    past_units = [(j, ct) for j in range(blocks_per_tile) for ct in range(n_ct)]

    def past_body(t, carry):
        base = blocks_per_tile * t
        for n, (j, ct) in enumerate(past_units):
            ahead = n + SCORE_LEAD
            j2, ct2 = past_units[ahead % len(past_units)]
            if ahead >= len(past_units):
                j2 += blocks_per_tile
            scores(base + j2, j2 % 2, ct2)
            attend(base + j, j % 2, ct)
        return carry

    own_units = [(j, ct) for j in range(blocks_per_tile) for ct in range(n_ct)
                 if (ct % ct_per_map) * MXU_COLS >= j * tk]

    for n in range(len(own_units)):
        ahead = n + SCORE_LEAD
        j2, ct2 = own_units[ahead] if ahead < len(own_units) else (0, ahead - len(own_units))
        readers = [i for i, (j, ct) in enumerate(own_units) if ct == ct2 and j % 2 == j2 % 2 and i < ahead]
        assert all(i < n for i in readers), (n, j2, ct2, readers)

    def tile_body(qi, carry):
        lax.fori_loop(0, qi, past_body, 0)
        base = blocks_per_tile * qi
        next_qi = jnp.minimum(qi + 1, n_q - 1)
        for n, (j, ct) in enumerate(own_units):
            ahead = n + SCORE_LEAD
            if ahead < len(own_units):
                j2, ct2 = own_units[ahead]
                scores(base + j2, j2 % 2, ct2)
            else:
                if ahead == len(own_units):
                    build_qq(next_qi)
                scores(0, 0, ahead - len(own_units))
            on_diagonal = (ct % ct_per_map) * MXU_COLS < (j + 1) * tk
            attend(base + j, j % 2, ct, key_offset=j * tk if on_diagonal else None)
            if on_diagonal and ct >= ct_per_map:
                finalize(qi, ct % ct_per_map)
        init_state()
        return carry

    build_qq(0)
    init_state()
    for ct in range(SCORE_LEAD):
        scores(0, 0, ct)
    lax.fori_loop(0, n_q, tile_body, 0)


def _prompt_attn(lq1, lk1, lq2, lk2, subln_g, q, kb, vt, *, n_seq, seq_len, lam_init):
    t = n_seq * seq_len
    nq = seq_len // ATTN_TQ
    nk = seq_len // ATTN_TK
    vec = _resident((1, HEAD_DIM))
    stat = pltpu.VMEM((1, 2 * ATTN_TQ), F32)
    score = pltpu.VMEM((ATTN_TK, 2 * ATTN_TQ), F32)
    head_rows = pl.BlockSpec((seq_len, HEAD_COLS), lambda b, h: (b, h))
    return pl.pallas_call(
        functools.partial(_prompt_attn_kernel, lam_init=lam_init, n_q=nq),
        grid=(n_seq, N_HEADS),
        in_specs=[vec, vec, vec, vec, _resident((1, V_DIM)),
                  pl.BlockSpec((None, None, seq_len // ROW_TILE, HEAD_COLS, ROW_TILE), lambda b, h: (b, h, 0, 0, 0)),
                  head_rows,
                  pl.BlockSpec((None, None, nk, VT_ROWS, ATTN_TK), lambda b, h: (b, h, 0, 0, 0))],
        out_specs=head_rows,
        out_shape=jax.ShapeDtypeStruct((t, N_HEADS * V_DIM), BF16),
        scratch_shapes=[pltpu.VMEM((HEAD_COLS, 2 * ATTN_TQ), BF16), score, score,
                        pltpu.VMEM((VT_ROWS, 2 * ATTN_TQ), F32), stat],
        compiler_params=_params(2),
        name="prompt_attn",
    )(lq1, lk1, lq2, lk2, subln_g, q, kb, vt)


def _sample_attn_kernel(lq1_ref, lk1_ref, lq2_ref, lk2_ref, sg_ref, q_ref, kn_ref, vn_ref, ck_ref, cv_ref,
                        o_ref, *, lam_init):
    tq = q_ref.shape[0]
    past = ck_ref.shape[1]
    lam = _lambda(lq1_ref, lk1_ref, lq2_ref, lk2_ref, lam_init)
    nt = (((1,), (1,)), ((), ()))
    for hh in range(N_HEADS):
        cols = slice(hh * HEAD_COLS, (hh + 1) * HEAD_COLS)
        qq = _split_maps(q_ref[:, cols])
        s_c = jnp.dot(qq, ck_ref[cols, :].astype(BF16), preferred_element_type=F32)
        s_n = lax.dot_general(qq, kn_ref[:, cols].astype(BF16), nt, preferred_element_type=F32)
        m = jnp.maximum(jnp.max(s_c, axis=-1, keepdims=True), jnp.max(s_n, axis=-1, keepdims=True))
        p_c = jnp.exp2(s_c - m)
        p_n = jnp.exp2(s_n - m)
        inv = 1.0 / (jnp.sum(p_c, axis=-1, keepdims=True) + jnp.sum(p_n, axis=-1, keepdims=True))
        p_c = p_c * inv
        p_n = p_n * inv
        a_c = (p_c[:tq] - lam * p_c[tq:]).astype(BF16)
        a_n = (p_n[:tq] - lam * p_n[tq:]).astype(BF16)
        v_c = cv_ref[pl.ds(hh, past, stride=N_HEADS), :]
        o = (jnp.dot(a_c, v_c.astype(BF16), preferred_element_type=F32)
             + jnp.dot(a_n, vn_ref[:, cols].astype(BF16), preferred_element_type=F32))
        o_ref[:, cols] = _subln(o, sg_ref[...], lam_init).astype(BF16)


def _sample_attn(lq1, lk1, lq2, lk2, subln_g, q, k_new, v_new, cache_k, cache_v, *, n_seq, seq_len, lam_init):
    past = cache_k.shape[2]
    vec = _resident((1, HEAD_DIM))
    new = pl.BlockSpec((seq_len, D_MODEL), lambda b: (b, 0))
    old_k = pl.BlockSpec((None, D_MODEL, past), lambda b: (b, 0, 0))
    old_v = pl.BlockSpec((None, past * N_HEADS, V_DIM), lambda b: (b, 0, 0))
    return pl.pallas_call(
        functools.partial(_sample_attn_kernel, lam_init=lam_init),
        grid=(n_seq,),
        in_specs=[vec, vec, vec, vec, _resident((1, V_DIM)), new, new, new, old_k, old_v],
        out_specs=new,
        out_shape=jax.ShapeDtypeStruct((n_seq * seq_len, D_MODEL), BF16),
        compiler_params=_params(1),
        name="sample_attn",
    )(lq1, lk1, lq2, lk2, subln_g, q, k_new, v_new, cache_k, cache_v)


def kernel(x_prompt, x_sample, cache_k, cache_v, state_conv, ffn1_norm_pre, ffn1_norm_post, ffn1_w_gate, ffn1_w_up, ffn1_w_down, mix_norm_pre, mix_norm_post, w_in, b_gate, lambda_q1, lambda_k1, lambda_q2, lambda_k2, subln_g, w_attn_out, conv_w, conv_b, w_conv_out, w_mix_out, ffn2_norm_pre, ffn2_norm_post, ffn2_w_gate, ffn2_w_up, ffn2_w_down):
    depth = w_in.shape[0]
    n_p, s_p, _ = x_prompt.shape
    n_s, s_s, _ = x_sample.shape
    past = cache_k.shape[2]
    assert s_p % ROW_TILE == 0 and ROW_TILE % ATTN_TK == 0 and ATTN_TK % CHUNK == 0
    assert s_p % ATTN_TQ == 0 and ATTN_TQ % (2 * ATTN_TK) == 0 and SCORE_LEAD <= ATTN_TQ // MXU_COLS
    assert ATTN_TK == MXU_COLS
    assert ROW_TILE % s_s == 0 and (n_s * s_s) % ROW_TILE == 0 and s_s >= CONV_W - 1

    xp = x_prompt.reshape(n_p * s_p, D_MODEL)
    xs = x_sample.reshape(n_s * s_s, D_MODEL)
    outs = {name: [] for name in ("kp", "vp", "cp", "ks", "vs", "cs")}
    for l in range(depth):
        lam_init = _lambda_init(l)
        vec = lambda a: a[l][None, :]
        bf = lambda a: a[l].astype(BF16)
        ffn1 = (vec(ffn1_norm_pre), vec(ffn1_norm_post), bf(ffn1_w_gate), bf(ffn1_w_up), bf(ffn1_w_down))
        ffn2 = (vec(ffn2_norm_pre), vec(ffn2_norm_post), bf(ffn2_w_gate), bf(ffn2_w_up), bf(ffn2_w_down))
        proj = (vec(mix_norm_pre), bf(w_in), vec(b_gate), conv_w[l], vec(conv_b), bf(w_conv_out))
        lam = (vec(lambda_q1), vec(lambda_k1), vec(lambda_q2), vec(lambda_k2), vec(subln_g))
        mix_w = (bf(w_attn_out), bf(w_mix_out), vec(mix_norm_post))

        xp = _ffn(xp, *ffn1)
        zero_state = jnp.zeros((n_p, CONV_W - 1, D_MODEL), F32)
        q, k, v, kb, vt, ga, gy, conv = _inproj(xp, *proj, zero_state, seq_len=s_p, emit_kv=True)
        o = _prompt_attn(*lam, q, kb, vt, n_seq=n_p, seq_len=s_p, lam_init=lam_init)
        xp = _mix_ffn(o, ga, gy, xp, *mix_w, *ffn2)
        outs["kp"].append(jnp.transpose(k.reshape(n_p, N_HEADS, 2, HEAD_DIM, s_p), (0, 4, 1, 2, 3)))
        outs["vp"].append(v.reshape(n_p, s_p, N_HEADS, V_DIM))
        outs["cp"].append(conv)

        xs = _ffn(xs, *ffn1)
        q, k, v, ga, gy, conv = _inproj(xs, *proj, state_conv[l], seq_len=s_s, emit_kv=False)
        cache_kt = jnp.transpose(cache_k[l], (0, 2, 3, 4, 1)).reshape(n_s, D_MODEL, past)
        cache_vr = cache_v[l].reshape(n_s, past * N_HEADS, V_DIM)
        o = _sample_attn(*lam, q, k, v, cache_kt, cache_vr, n_seq=n_s, seq_len=s_s, lam_init=lam_init)
        xs = _mix_ffn(o, ga, gy, xs, *mix_w, *ffn2)
        outs["ks"].append(k.reshape(n_s, s_s, N_HEADS, 2, HEAD_DIM))
        outs["vs"].append(v.reshape(n_s, s_s, N_HEADS, V_DIM))
        outs["cs"].append(conv)

    return (xp.reshape(n_p, s_p, D_MODEL), xs.reshape(n_s, s_s, D_MODEL),
            jnp.stack(outs["kp"]), jnp.stack(outs["vp"]), jnp.stack(outs["cp"]),
            jnp.stack(outs["ks"]), jnp.stack(outs["vs"]), jnp.stack(outs["cs"]))
```

```python
import functools
import math

import jax
import jax.numpy as jnp
from jax import lax
from jax.experimental import pallas as pl
from jax.experimental.pallas import tpu as pltpu

D_MODEL = 1024
N_HEADS = 8
HEAD_DIM = 64
V_DIM = 2 * HEAD_DIM
HEAD_COLS = 2 * HEAD_DIM
D_FF = 2816
CHUNK = 64
CONV_W = 3
EPS = 1e-6
NEG = -1e30
LOG2E = 1.4426950408889634
Q_SCALE = (HEAD_DIM ** -0.5) * LOG2E

V7X_VMEM_LIMIT_BYTES = 58 * 1024 * 1024
ROW_TILE = 512
MXU_COLS = 256
LANES = 128
FF_CHUNK = MXU_COLS
ATTN_TQ = 2048
ATTN_TK = 256
SCORE_LEAD = 4
BF16_SUBLANES = 16
VT_ROWS = V_DIM + BF16_SUBLANES

F32 = jnp.float32
BF16 = jnp.bfloat16


def _lambda_init(layer_idx):
    return 0.8 - 0.6 * math.exp(-0.3 * layer_idx)


def _rms(x, g):
    return x * lax.rsqrt(jnp.mean(x * x, axis=-1, keepdims=True) + EPS) * g


def _resident(shape):
    zeros = (0,) * len(shape)
    return pl.BlockSpec(shape, lambda *_: zeros, pipeline_mode=pl.Buffered(1))


def _params(n_axes):
    return pltpu.CompilerParams(
        dimension_semantics=("arbitrary",) * n_axes,
        vmem_limit_bytes=V7X_VMEM_LIMIT_BYTES,
    )


def _swiglu_residual(x, gpre_ref, gpost_ref, wg_ref, wu_ref, wd_ref, a_scr):
    h = _rms(x, gpre_ref[...]).astype(BF16)
    for c in range(0, D_FF, FF_CHUNK):
        cols = slice(c, min(c + FF_CHUNK, D_FF))
        g = jnp.dot(h, wg_ref[:, cols], preferred_element_type=F32)
        u = jnp.dot(h, wu_ref[:, cols], preferred_element_type=F32)
        a_scr[:, cols] = (g * jax.nn.sigmoid(g) * u).astype(BF16)
    y = jnp.dot(a_scr[...], wd_ref[...], preferred_element_type=F32)
    return x + 0.5 * _rms(y, gpost_ref[...])


def _ffn_kernel(x_ref, gpre_ref, gpost_ref, wg_ref, wu_ref, wd_ref, o_ref, a_scr):
    o_ref[...] = _swiglu_residual(x_ref[...], gpre_ref, gpost_ref, wg_ref, wu_ref, wd_ref, a_scr)


def _mix_ffn_kernel(o_ref, ga_ref, gy_ref, x_ref, wa_ref, wm_ref, gmix_ref,
                    gpre_ref, gpost_ref, wg_ref, wu_ref, wd_ref, out_ref, a_scr):
    y_attn = jnp.dot(o_ref[...], wa_ref[...], preferred_element_type=F32)
    merged = (ga_ref[...].astype(F32) * y_attn + gy_ref[...].astype(F32)).astype(BF16)
    y = jnp.dot(merged, wm_ref[...], preferred_element_type=F32)
    x = x_ref[...] + _rms(y, gmix_ref[...])
    out_ref[...] = _swiglu_residual(x, gpre_ref, gpost_ref, wg_ref, wu_ref, wd_ref, a_scr)


_FFN_WEIGHT_SPECS = (((1, D_MODEL)), ((1, D_MODEL)), (D_MODEL, D_FF), (D_MODEL, D_FF), (D_FF, D_MODEL))


def _ffn(x, g_pre, g_post, w_gate, w_up, w_down):
    t = x.shape[0]
    row = pl.BlockSpec((ROW_TILE, D_MODEL), lambda i: (i, 0))
    return pl.pallas_call(
        _ffn_kernel,
        grid=(t // ROW_TILE,),
        in_specs=[row] + [_resident(s) for s in _FFN_WEIGHT_SPECS],
        out_specs=row,
        out_shape=jax.ShapeDtypeStruct((t, D_MODEL), F32),
        scratch_shapes=[pltpu.VMEM((ROW_TILE, D_FF), BF16)],
        compiler_params=_params(1),
        name="ffn",
    )(x, g_pre, g_post, w_gate, w_up, w_down)


def _mix_ffn(o, gate_a, gated_conv, x, w_attn_out, w_mix_out, g_mix, g_pre, g_post, w_gate, w_up, w_down):
    t = x.shape[0]
    row = pl.BlockSpec((ROW_TILE, D_MODEL), lambda i: (i, 0))
    return pl.pallas_call(
        _mix_ffn_kernel,
        grid=(t // ROW_TILE,),
        in_specs=[row, row, row, row, _resident((D_MODEL, D_MODEL)), _resident((D_MODEL, D_MODEL)),
                  _resident((1, D_MODEL))] + [_resident(s) for s in _FFN_WEIGHT_SPECS],
        out_specs=row,
        out_shape=jax.ShapeDtypeStruct((t, D_MODEL), F32),
        scratch_shapes=[pltpu.VMEM((ROW_TILE, D_FF), BF16)],
        compiler_params=_params(1),
        name="mix_ffn",
    )(o, gate_a, gated_conv, x, w_attn_out, w_mix_out, g_mix, g_pre, g_post, w_gate, w_up, w_down)


def _inproj_kernel(x_ref, gpre_ref, w_ref, bg_ref, cw_ref, cb_ref, wco_ref, st_ref, *refs,
                   nb, lt, tiles_per_seq, emit_kv):
    if emit_kv:
        q_ref, k_ref, v_ref, kb_ref, vt_ref, ga_ref, gy_ref, nc_ref, carry_scr = refs
    else:
        q_ref, k_ref, v_ref, ga_ref, gy_ref, nc_ref, carry_scr = refs
    i = pl.program_id(0)
    h = _rms(x_ref[...], gpre_ref[...]).astype(BF16)

    def seg(j):
        return jnp.dot(h, w_ref[:, j * D_MODEL:(j + 1) * D_MODEL], preferred_element_type=F32)

    q = seg(0) * Q_SCALE
    if emit_kv:
        for hh in range(N_HEADS):
            q_ref[hh] = q[:, hh * HEAD_COLS:(hh + 1) * HEAD_COLS].T.astype(BF16)
    else:
        q_ref[...] = q.astype(BF16)
    k = seg(1)
    k_ref[...] = k.T if emit_kv else k
    v = seg(2)
    v_ref[...] = v
    if emit_kv:
        kb_ref[...] = k.astype(BF16)
        for hh in range(N_HEADS):
            for kb in range(ROW_TILE // ATTN_TK):
                vt_ref[hh, kb, :V_DIM, :] = (
                    v[kb * ATTN_TK:(kb + 1) * ATTN_TK, hh * V_DIM:(hh + 1) * V_DIM].T.astype(BF16))
                vt_ref[hh, kb, V_DIM:, :] = jnp.ones((BF16_SUBLANES, ATTN_TK), BF16)

    @pl.when(i % tiles_per_seq == 0)
    def _():
        carry_scr[...] = st_ref[...]

    gate_b = seg(3).reshape(nb, lt, D_MODEL)
    u = (seg(4) * seg(5)).reshape(nb, lt, D_MODEL)
    prev = carry_scr[...]
    prev0 = prev[:, 0:1, :]
    prev1 = prev[:, 1:2, :]
    pos = lax.broadcasted_iota(jnp.int32, u.shape, 1)
    back1 = jnp.where(pos == 0, prev1, pltpu.roll(u, 1, 1))
    back2 = jnp.where(pos == 0, prev0, jnp.where(pos == 1, prev1, pltpu.roll(u, 2, 1)))
    cw = cw_ref[...]
    zc = cb_ref[...] + cw[0:1, :] * back2 + cw[1:2, :] * back1 + cw[2:3, :] * u
    tail = u[:, lt - (CONV_W - 1):, :]
    carry_scr[...] = tail
    nc_ref[...] = tail
    c_in = (gate_b * zc).reshape(nb * lt, D_MODEL).astype(BF16)
    y_conv = jnp.dot(c_in, wco_ref[...], preferred_element_type=F32)

    bg = bg_ref[...]
    ga_ref[...] = jax.nn.sigmoid(seg(6) + bg[:, :D_MODEL]).astype(BF16)
    gy_ref[...] = (jax.nn.sigmoid(seg(7) + bg[:, D_MODEL:]) * y_conv).astype(BF16)


def _inproj(x, g_pre, w_in, b_gate, conv_w, conv_b, w_conv_out, state, *, seq_len, emit_kv):
    t = x.shape[0]
    n_seq = t // seq_len
    lt = min(seq_len, ROW_TILE)
    nb = ROW_TILE // lt
    tiles_per_seq = seq_len // lt
    n_tiles = t // ROW_TILE
    row = lambda dt: (pl.BlockSpec((ROW_TILE, D_MODEL), lambda i: (i, 0)),
                      jax.ShapeDtypeStruct((t, D_MODEL), dt))
    seq_block = pl.BlockSpec((nb, CONV_W - 1, D_MODEL), lambda i: (i // tiles_per_seq, 0, 0))
    outs = [row(BF16), row(F32), row(F32)]
    if emit_kv:
        outs[0] = (pl.BlockSpec((None, N_HEADS, None, HEAD_COLS, ROW_TILE),
                                lambda i: (i // tiles_per_seq, 0, i % tiles_per_seq, 0, 0)),
                   jax.ShapeDtypeStruct((n_seq, N_HEADS, tiles_per_seq, HEAD_COLS, ROW_TILE), BF16))
        outs[1] =(pl.BlockSpec((None, D_MODEL, ROW_TILE), lambda i: (i // tiles_per_seq, 0, i % tiles_per_seq)),
                   jax.ShapeDtypeStruct((n_seq, D_MODEL, seq_len), F32))
        outs.append(row(BF16))
        kb_per_tile = ROW_TILE // ATTN_TK
        outs.append((pl.BlockSpec((None, N_HEADS, kb_per_tile, VT_ROWS, ATTN_TK),
                                  lambda i: (i // tiles_per_seq, 0, i % tiles_per_seq, 0, 0)),
                     jax.ShapeDtypeStruct((n_seq, N_HEADS, tiles_per_seq * kb_per_tile, VT_ROWS, ATTN_TK), BF16)))
    outs += [row(BF16), row(BF16),
             (seq_block, jax.ShapeDtypeStruct((n_seq, CONV_W - 1, D_MODEL), F32))]
    kern = functools.partial(_inproj_kernel, nb=nb, lt=lt, tiles_per_seq=tiles_per_seq, emit_kv=emit_kv)
    return pl.pallas_call(
        kern,
        grid=(n_tiles,),
        in_specs=[pl.BlockSpec((ROW_TILE, D_MODEL), lambda i: (i, 0)),
                  _resident((1, D_MODEL)), _resident((D_MODEL, 8 * D_MODEL)), _resident((1, 2 * D_MODEL)),
                  _resident((CONV_W, D_MODEL)), _resident((1, D_MODEL)), _resident((D_MODEL, D_MODEL)),
                  seq_block],
        out_specs=[o[0] for o in outs],
        out_shape=[o[1] for o in outs],
        scratch_shapes=[pltpu.VMEM((nb, CONV_W - 1, D_MODEL), F32)],
        compiler_params=_params(1),
        name="inproj",
    )(x, g_pre, w_in, b_gate, conv_w, conv_b, w_conv_out, state)


def _lambda(lq1_ref, lk1_ref, lq2_ref, lk2_ref, lam_init):
    a = jnp.sum(lq1_ref[...] * lk1_ref[...], axis=-1, keepdims=True)
    b = jnp.sum(lq2_ref[...] * lk2_ref[...], axis=-1, keepdims=True)
    return jnp.exp(a) - jnp.exp(b) + lam_init


def _split_maps(qb):
    lane = lax.broadcasted_iota(jnp.int32, qb.shape, 1)
    zero = jnp.zeros_like(qb)
    return jnp.concatenate([jnp.where(lane < HEAD_DIM, qb, zero),
                            jnp.where(lane >= HEAD_DIM, qb, zero)], axis=0)


def _subln(o, g, lam_init):
    return (o * lax.rsqrt(jnp.mean(o * o, axis=-1, keepdims=True) + EPS) * g) * (1.0 - lam_init)


def _prompt_attn_kernel(lq1_ref, lk1_ref, lq2_ref, lk2_ref, sg_ref, q_ref, k_ref, vt_ref, o_ref,
                        qq_scr, s_a1, s_a2, s_b1, s_b2, acc_scr, m_scr, *, lam_init, n_q):
    tq, tk = ATTN_TQ, ATTN_TK
    bufs = ((s_a1, s_a2), (s_b1, s_b2))
    n_ct = 2 * tq // MXU_COLS
    ct_per_map = tq // MXU_COLS
    blocks_per_tile = tq // tk
    lam = _lambda(lq1_ref, lk1_ref, lq2_ref, lk2_ref, lam_init)

    def build_qq(qi):
        parts = tq // ROW_TILE
        for part in range(parts):
            q_t = q_ref[parts * qi + part]
            row = lax.broadcasted_iota(jnp.int32, q_t.shape, 0)
            zero = jnp.zeros_like(q_t)
            lo = part * ROW_TILE
            qq_scr[:, lo:lo + ROW_TILE] = jnp.where(row < HEAD_DIM, q_t, zero)
            qq_scr[:, tq + lo:tq + lo + ROW_TILE] = jnp.where(row >= HEAD_DIM, q_t, zero)

    def init_state():
        m_scr[...] = jnp.full(m_scr.shape, NEG, F32)
        acc_scr[...] = jnp.zeros(acc_scr.shape, F32)

    def scores(blk, parity, ct):
        cols = slice(ct * MXU_COLS, (ct + 1) * MXU_COLS)
        local = (ct % ct_per_map) * MXU_COLS
        k_blk = k_ref[pl.ds(pl.multiple_of(blk * tk, tk), tk), :]
        bufs[parity][ct // ct_per_map][:, local:local + MXU_COLS] = jnp.dot(
            k_blk, qq_scr[:, cols], preferred_element_type=F32)

    def softmax(s_maps, c, key_offset):
        strip = slice(c, c + LANES)
        s = s_maps[c // tq][:, c % tq:c % tq + LANES]
        if key_offset is not None:
            kc = (lax.broadcasted_iota(jnp.int32, s.shape, 0) + key_offset) // CHUNK
            qc = (lax.broadcasted_iota(jnp.int32, s.shape, 1) + (c % tq)) // CHUNK
            s = jnp.where(kc <= qc, s, NEG)
        m_prev = m_scr[:, strip]
        m_new = jnp.maximum(m_prev, jnp.max(s, axis=0, keepdims=True))
        alpha = jnp.exp2(m_prev - m_new)
        p = jnp.exp2(s - m_new)
        m_scr[:, strip] = m_new
        return p.astype(BF16), alpha

    def attend(blk, parity, ct, key_offset=None):
        c = ct * MXU_COLS
        cols = slice(c, c + MXU_COLS)
        strips = [softmax(bufs[parity], cc, key_offset) for cc in range(c, c + MXU_COLS, LANES)]
        p = jnp.concatenate([st[0] for st in strips], axis=1)
        alpha = jnp.concatenate([st[1] for st in strips], axis=1)
        pv = jnp.dot(vt_ref[blk], p, preferred_element_type=F32)
        acc_scr[:, cols] = alpha * acc_scr[:, cols] + pv

    def finalize(qi, part):
        lo = part * MXU_COLS
        o1 = acc_scr[:V_DIM, lo:lo + MXU_COLS] * (1.0 / acc_scr[V_DIM:V_DIM + 1, lo:lo + MXU_COLS])
        o2 = (acc_scr[:V_DIM, tq + lo:tq + lo + MXU_COLS]
              * (1.0 / acc_scr[V_DIM:V_DIM + 1, tq + lo:tq + lo + MXU_COLS]))
        o_t = o1 - lam * o2
        rows = pl.ds(pl.multiple_of(qi * tq + lo, MXU_COLS), MXU_COLS)
        o_ref[rows, :] = _subln(o_t.T, sg_ref[...], lam_init).astype(BF16)

    past_units = [(j, ct) for j in range(blocks_per_tile) for ct in range(n_ct)]

    def past_body(t, carry):
        base = blocks_per_tile * t
        for n, (j, ct) in enumerate(past_units):
            ahead = n + SCORE_LEAD
            j2, ct2 = past_units[ahead % len(past_units)]
            if ahead >= len(past_units):
                j2 += blocks_per_tile
            scores(base + j2, j2 % 2, ct2)
            attend(base + j, j % 2, ct)
        return carry

    own_units = [(j, ct) for j in range(blocks_per_tile) for ct in range(n_ct)
                 if (ct % ct_per_map) * MXU_COLS >= j * tk]

    for n in range(len(own_units)):
        ahead = n + SCORE_LEAD
        j2, ct2 = own_units[ahead] if ahead < len(own_units) else (0, ahead - len(own_units))
        readers = [i for i, (j, ct) in enumerate(own_units) if ct == ct2 and j % 2 == j2 % 2 and i < ahead]
        assert all(i < n for i in readers), (n, j2, ct2, readers)

    def tile_body(qi, carry):
        lax.fori_loop(0, qi, past_body, 0)
        base = blocks_per_tile * qi
        next_qi = jnp.minimum(qi + 1, n_q - 1)
        for n, (j, ct) in enumerate(own_units):
            ahead = n + SCORE_LEAD
            if ahead < len(own_units):
                j2, ct2 = own_units[ahead]
                scores(base + j2, j2 % 2, ct2)
            else:
                if ahead == len(own_units):
                    build_qq(next_qi)
                scores(0, 0, ahead - len(own_units))
            on_diagonal = (ct % ct_per_map) * MXU_COLS < (j + 1) * tk
            attend(base + j, j % 2, ct, key_offset=j * tk if on_diagonal else None)
            if on_diagonal and ct >= ct_per_map:
                finalize(qi, ct % ct_per_map)
        init_state()
        return carry

    build_qq(0)
    init_state()
    for ct in range(SCORE_LEAD):
        scores(0, 0, ct)
    lax.fori_loop(0, n_q, tile_body, 0)


def _prompt_attn(lq1, lk1, lq2, lk2, subln_g, q, kb, vt, *, n_seq, seq_len, lam_init):
    t = n_seq * seq_len
    nq = seq_len // ATTN_TQ
    nk = seq_len // ATTN_TK
    vec = _resident((1, HEAD_DIM))
    stat = pltpu.VMEM((1, 2 * ATTN_TQ), F32)
    score = pltpu.VMEM((ATTN_TK, ATTN_TQ + LANES), F32)
    head_rows = pl.BlockSpec((seq_len, HEAD_COLS), lambda b, h: (b, h))
    return pl.pallas_call(
        functools.partial(_prompt_attn_kernel, lam_init=lam_init, n_q=nq),
        grid=(n_seq, N_HEADS),
        in_specs=[vec, vec, vec, vec, _resident((1, V_DIM)),
                  pl.BlockSpec((None, None, seq_len // ROW_TILE, HEAD_COLS, ROW_TILE), lambda b, h: (b, h, 0, 0, 0)),
                  head_rows,
                  pl.BlockSpec((None, None, nk, VT_ROWS, ATTN_TK), lambda b, h: (b, h, 0, 0, 0))],
        out_specs=head_rows,
        out_shape=jax.ShapeDtypeStruct((t, N_HEADS * V_DIM), BF16),
        scratch_shapes=[pltpu.VMEM((HEAD_COLS, 2 * ATTN_TQ), BF16), score, score, score, score,
                        pltpu.VMEM((VT_ROWS, 2 * ATTN_TQ), F32), stat],
        compiler_params=_params(2),
        name="prompt_attn",
    )(lq1, lk1, lq2, lk2, subln_g, q, kb, vt)


def _sample_attn_kernel(lq1_ref, lk1_ref, lq2_ref, lk2_ref, sg_ref, q_ref, kn_ref, vn_ref, ck_ref, cv_ref,
                        o_ref, *, lam_init):
    tq = q_ref.shape[0]
    past = ck_ref.shape[1]
    lam = _lambda(lq1_ref, lk1_ref, lq2_ref, lk2_ref, lam_init)
    nt = (((1,), (1,)), ((), ()))
    for hh in range(N_HEADS):
        cols = slice(hh * HEAD_COLS, (hh + 1) * HEAD_COLS)
        qq = _split_maps(q_ref[:, cols])
        s_c = jnp.dot(qq, ck_ref[cols, :].astype(BF16), preferred_element_type=F32)
        s_n = lax.dot_general(qq, kn_ref[:, cols].astype(BF16), nt, preferred_element_type=F32)
        m = jnp.maximum(jnp.max(s_c, axis=-1, keepdims=True), jnp.max(s_n, axis=-1, keepdims=True))
        p_c = jnp.exp2(s_c - m)
        p_n = jnp.exp2(s_n - m)
        inv = 1.0 / (jnp.sum(p_c, axis=-1, keepdims=True) + jnp.sum(p_n, axis=-1, keepdims=True))
        p_c = p_c * inv
        p_n = p_n * inv
        a_c = (p_c[:tq] - lam * p_c[tq:]).astype(BF16)
        a_n = (p_n[:tq] - lam * p_n[tq:]).astype(BF16)
        v_c = cv_ref[pl.ds(hh, past, stride=N_HEADS), :]
        o = (jnp.dot(a_c, v_c.astype(BF16), preferred_element_type=F32)
             + jnp.dot(a_n, vn_ref[:, cols].astype(BF16), preferred_element_type=F32))
        o_ref[:, cols] = _subln(o, sg_ref[...], lam_init).astype(BF16)


def _sample_attn(lq1, lk1, lq2, lk2, subln_g, q, k_new, v_new, cache_k, cache_v, *, n_seq, seq_len, lam_init):
    past = cache_k.shape[2]
    vec = _resident((1, HEAD_DIM))
    new = pl.BlockSpec((seq_len, D_MODEL), lambda b: (b, 0))
    old_k = pl.BlockSpec((None, D_MODEL, past), lambda b: (b, 0, 0))
    old_v = pl.BlockSpec((None, past * N_HEADS, V_DIM), lambda b: (b, 0, 0))
    return pl.pallas_call(
        functools.partial(_sample_attn_kernel, lam_init=lam_init),
        grid=(n_seq,),
        in_specs=[vec, vec, vec, vec, _resident((1, V_DIM)), new, new, new, old_k, old_v],
        out_specs=new,
        out_shape=jax.ShapeDtypeStruct((n_seq * seq_len, D_MODEL), BF16),
        compiler_params=_params(1),
        name="sample_attn",
    )(lq1, lk1, lq2, lk2, subln_g, q, k_new, v_new, cache_k, cache_v)


def kernel(x_prompt, x_sample, cache_k, cache_v, state_conv, ffn1_norm_pre, ffn1_norm_post, ffn1_w_gate, ffn1_w_up, ffn1_w_down, mix_norm_pre, mix_norm_post, w_in, b_gate, lambda_q1, lambda_k1, lambda_q2, lambda_k2, subln_g, w_attn_out, conv_w, conv_b, w_conv_out, w_mix_out, ffn2_norm_pre, ffn2_norm_post, ffn2_w_gate, ffn2_w_up, ffn2_w_down):
    depth = w_in.shape[0]
    n_p, s_p, _ = x_prompt.shape
    n_s, s_s, _ = x_sample.shape
    past = cache_k.shape[2]
    assert s_p % ROW_TILE == 0 and ROW_TILE % ATTN_TK == 0 and ATTN_TK % CHUNK == 0
    assert s_p % ATTN_TQ == 0 and ATTN_TQ % (2 * ATTN_TK) == 0 and SCORE_LEAD <= ATTN_TQ // MXU_COLS
    assert ATTN_TK == MXU_COLS
    assert ROW_TILE % s_s == 0 and (n_s * s_s) % ROW_TILE == 0 and s_s >= CONV_W - 1

    xp = x_prompt.reshape(n_p * s_p, D_MODEL)
    xs = x_sample.reshape(n_s * s_s, D_MODEL)
    outs = {name: [] for name in ("kp", "vp", "cp", "ks", "vs", "cs")}
    for l in range(depth):
        lam_init = _lambda_init(l)
        vec = lambda a: a[l][None, :]
        bf = lambda a: a[l].astype(BF16)
        ffn1 = (vec(ffn1_norm_pre), vec(ffn1_norm_post), bf(ffn1_w_gate), bf(ffn1_w_up), bf(ffn1_w_down))
        ffn2 = (vec(ffn2_norm_pre), vec(ffn2_norm_post), bf(ffn2_w_gate), bf(ffn2_w_up), bf(ffn2_w_down))
        proj = (vec(mix_norm_pre), bf(w_in), vec(b_gate), conv_w[l], vec(conv_b), bf(w_conv_out))
        lam = (vec(lambda_q1), vec(lambda_k1), vec(lambda_q2), vec(lambda_k2), vec(subln_g))
        mix_w = (bf(w_attn_out), bf(w_mix_out), vec(mix_norm_post))

        xp = _ffn(xp, *ffn1)
        zero_state = jnp.zeros((n_p, CONV_W - 1, D_MODEL), F32)
        q, k, v, kb, vt, ga, gy, conv = _inproj(xp, *proj, zero_state, seq_len=s_p, emit_kv=True)
        o = _prompt_attn(*lam, q, kb, vt, n_seq=n_p, seq_len=s_p, lam_init=lam_init)
        xp = _mix_ffn(o, ga, gy, xp, *mix_w, *ffn2)
        outs["kp"].append(jnp.transpose(k.reshape(n_p, N_HEADS, 2, HEAD_DIM, s_p), (0, 4, 1, 2, 3)))
        outs["vp"].append(v.reshape(n_p, s_p, N_HEADS, V_DIM))
        outs["cp"].append(conv)

        xs = _ffn(xs, *ffn1)
        q, k, v, ga, gy, conv = _inproj(xs, *proj, state_conv[l], seq_len=s_s, emit_kv=False)
        cache_kt = jnp.transpose(cache_k[l], (0, 2, 3, 4, 1)).reshape(n_s, D_MODEL, past)
        cache_vr = cache_v[l].reshape(n_s, past * N_HEADS, V_DIM)
        o = _sample_attn(*lam, q, k, v, cache_kt, cache_vr, n_seq=n_s, seq_len=s_s, lam_init=lam_init)
        xs = _mix_ffn(o, ga, gy, xs, *mix_w, *ffn2)
        outs["ks"].append(k.reshape(n_s, s_s, N_HEADS, 2, HEAD_DIM))
        outs["vs"].append(v.reshape(n_s, s_s, N_HEADS, V_DIM))
        outs["cs"].append(conv)

    return (xp.reshape(n_p, s_p, D_MODEL), xs.reshape(n_s, s_s, D_MODEL),
            jnp.stack(outs["kp"]), jnp.stack(outs["vp"]), jnp.stack(outs["cp"]),
            jnp.stack(outs["ks"]), jnp.stack(outs["vs"]), jnp.stack(outs["cs"]))
```

```python
import functools
import math

import jax
import jax.numpy as jnp
from jax import lax
from jax.experimental import pallas as pl
from jax.experimental.pallas import tpu as pltpu

D_MODEL = 1024
N_HEADS = 8
HEAD_DIM = 64
V_DIM = 2 * HEAD_DIM
HEAD_COLS = 2 * HEAD_DIM
D_FF = 2816
CHUNK = 64
CONV_W = 3
EPS = 1e-6
NEG = -1e30
LOG2E = 1.4426950408889634
Q_SCALE = (HEAD_DIM ** -0.5) * LOG2E

V7X_VMEM_LIMIT_BYTES = 58 * 1024 * 1024
ROW_TILE = 512
MXU_COLS = 256
FF_CHUNK = MXU_COLS
ATTN_TQ = 2048
ATTN_TK = 256
SCORE_LEAD = 4
SAMPLE_TK = 256
BF16_SUBLANES = 16
VT_ROWS = V_DIM + BF16_SUBLANES

F32 = jnp.float32
BF16 = jnp.bfloat16


def _lambda_init(layer_idx):
    return 0.8 - 0.6 * math.exp(-0.3 * layer_idx)


def _rms(x, g):
    return x * lax.rsqrt(jnp.mean(x * x, axis=-1, keepdims=True) + EPS) * g


def _resident(shape):
    zeros = (0,) * len(shape)
    return pl.BlockSpec(shape, lambda *_: zeros, pipeline_mode=pl.Buffered(1))


def _params(n_axes):
    return pltpu.CompilerParams(
        dimension_semantics=("arbitrary",) * n_axes,
        vmem_limit_bytes=V7X_VMEM_LIMIT_BYTES,
    )


def _swiglu_residual(x, gpre_ref, gpost_ref, wg_ref, wu_ref, wd_ref, a_scr):
    h = _rms(x, gpre_ref[...]).astype(BF16)
    for c in range(0, D_FF, FF_CHUNK):
        cols = slice(c, min(c + FF_CHUNK, D_FF))
        g = jnp.dot(h, wg_ref[:, cols], preferred_element_type=F32)
        u = jnp.dot(h, wu_ref[:, cols], preferred_element_type=F32)
        a_scr[:, cols] = (g * jax.nn.sigmoid(g) * u).astype(BF16)
    y = jnp.dot(a_scr[...], wd_ref[...], preferred_element_type=F32)
    return x + 0.5 * _rms(y, gpost_ref[...])


def _ffn_kernel(x_ref, gpre_ref, gpost_ref, wg_ref, wu_ref, wd_ref, o_ref, a_scr):
    o_ref[...] = _swiglu_residual(x_ref[...], gpre_ref, gpost_ref, wg_ref, wu_ref, wd_ref, a_scr)


def _mix_ffn_kernel(o_ref, ga_ref, gy_ref, x_ref, wa_ref, wm_ref, gmix_ref,
                    gpre_ref, gpost_ref, wg_ref, wu_ref, wd_ref, out_ref, a_scr):
    y_attn = jnp.dot(o_ref[...], wa_ref[...], preferred_element_type=F32)
    merged = (ga_ref[...].astype(F32) * y_attn + gy_ref[...].astype(F32)).astype(BF16)
    y = jnp.dot(merged, wm_ref[...], preferred_element_type=F32)
    x = x_ref[...] + _rms(y, gmix_ref[...])
    out_ref[...] = _swiglu_residual(x, gpre_ref, gpost_ref, wg_ref, wu_ref, wd_ref, a_scr)


_FFN_WEIGHT_SPECS = (((1, D_MODEL)), ((1, D_MODEL)), (D_MODEL, D_FF), (D_MODEL, D_FF), (D_FF, D_MODEL))


def _ffn(x, g_pre, g_post, w_gate, w_up, w_down):
    t = x.shape[0]
    row = pl.BlockSpec((ROW_TILE, D_MODEL), lambda i: (i, 0))
    return pl.pallas_call(
        _ffn_kernel,
        grid=(t // ROW_TILE,),
        in_specs=[row] + [_resident(s) for s in _FFN_WEIGHT_SPECS],
        out_specs=row,
        out_shape=jax.ShapeDtypeStruct((t, D_MODEL), F32),
        scratch_shapes=[pltpu.VMEM((ROW_TILE, D_FF), BF16)],
        compiler_params=_params(1),
        name="ffn",
    )(x, g_pre, g_post, w_gate, w_up, w_down)


def _mix_ffn(o, gate_a, gated_conv, x, w_attn_out, w_mix_out, g_mix, g_pre, g_post, w_gate, w_up, w_down):
    t = x.shape[0]
    row = pl.BlockSpec((ROW_TILE, D_MODEL), lambda i: (i, 0))
    return pl.pallas_call(
        _mix_ffn_kernel,
        grid=(t // ROW_TILE,),
        in_specs=[row, row, row, row, _resident((D_MODEL, D_MODEL)), _resident((D_MODEL, D_MODEL)),
                  _resident((1, D_MODEL))] + [_resident(s) for s in _FFN_WEIGHT_SPECS],
        out_specs=row,
        out_shape=jax.ShapeDtypeStruct((t, D_MODEL), F32),
        scratch_shapes=[pltpu.VMEM((ROW_TILE, D_FF), BF16)],
        compiler_params=_params(1),
        name="mix_ffn",
    )(o, gate_a, gated_conv, x, w_attn_out, w_mix_out, g_mix, g_pre, g_post, w_gate, w_up, w_down)


def _inproj_kernel(x_ref, gpre_ref, w_ref, bg_ref, cw_ref, cb_ref, wco_ref, st_ref, *refs,
                   nb, lt, tiles_per_seq, emit_kv):
    if emit_kv:
        q_ref, k_ref, v_ref, kb_ref, vt_ref, ga_ref, gy_ref, nc_ref, carry_scr = refs
    else:
        q_ref, k_ref, v_ref, ga_ref, gy_ref, nc_ref, carry_scr = refs
    i = pl.program_id(0)
    h = _rms(x_ref[...], gpre_ref[...]).astype(BF16)

    def seg(j):
        return jnp.dot(h, w_ref[:, j * D_MODEL:(j + 1) * D_MODEL], preferred_element_type=F32)

    q = seg(0) * Q_SCALE
    if emit_kv:
        for hh in range(N_HEADS):
            q_ref[hh] = q[:, hh * HEAD_COLS:(hh + 1) * HEAD_COLS].T.astype(BF16)
    else:
        q_ref[...] = q.astype(BF16)
    k = seg(1)
    k_ref[...] = k.T if emit_kv else k
    v = seg(2)
    v_ref[...] = v
    if emit_kv:
        kb_ref[...] = k.astype(BF16)
        for hh in range(N_HEADS):
            for kb in range(ROW_TILE // ATTN_TK):
                vt_ref[hh, kb, :V_DIM, :] = (
                    v[kb * ATTN_TK:(kb + 1) * ATTN_TK, hh * V_DIM:(hh + 1) * V_DIM].T.astype(BF16))
                vt_ref[hh, kb, V_DIM:, :] = jnp.ones((BF16_SUBLANES, ATTN_TK), BF16)

    @pl.when(i % tiles_per_seq == 0)
    def _():
        carry_scr[...] = st_ref[...]

    gate_b = seg(3).reshape(nb, lt, D_MODEL)
    u = (seg(4) * seg(5)).reshape(nb, lt, D_MODEL)
    prev = carry_scr[...]
    prev0 = prev[:, 0:1, :]
    prev1 = prev[:, 1:2, :]
    pos = lax.broadcasted_iota(jnp.int32, u.shape, 1)
    back1 = jnp.where(pos == 0, prev1, pltpu.roll(u, 1, 1))
    back2 = jnp.where(pos == 0, prev0, jnp.where(pos == 1, prev1, pltpu.roll(u, 2, 1)))
    cw = cw_ref[...]
    zc = cb_ref[...] + cw[0:1, :] * back2 + cw[1:2, :] * back1 + cw[2:3, :] * u
    tail = u[:, lt - (CONV_W - 1):, :]
    carry_scr[...] = tail
    nc_ref[...] = tail
    c_in = (gate_b * zc).reshape(nb * lt, D_MODEL).astype(BF16)
    y_conv = jnp.dot(c_in, wco_ref[...], preferred_element_type=F32)

    bg = bg_ref[...]
    ga_ref[...] = jax.nn.sigmoid(seg(6) + bg[:, :D_MODEL]).astype(BF16)
    gy_ref[...] = (jax.nn.sigmoid(seg(7) + bg[:, D_MODEL:]) * y_conv).astype(BF16)


def _inproj(x, g_pre, w_in, b_gate, conv_w, conv_b, w_conv_out, state, *, seq_len, emit_kv):
    t = x.shape[0]
    n_seq = t // seq_len
    lt = min(seq_len, ROW_TILE)
    nb = ROW_TILE // lt
    tiles_per_seq = seq_len // lt
    n_tiles = t // ROW_TILE
    row = lambda dt: (pl.BlockSpec((ROW_TILE, D_MODEL), lambda i: (i, 0)),
                      jax.ShapeDtypeStruct((t, D_MODEL), dt))
    seq_block = pl.BlockSpec((nb, CONV_W - 1, D_MODEL), lambda i: (i // tiles_per_seq, 0, 0))
    outs = [row(BF16), row(F32), row(F32)]
    if emit_kv:
        outs[0] = (pl.BlockSpec((None, N_HEADS, None, HEAD_COLS, ROW_TILE),
                                lambda i: (i // tiles_per_seq, 0, i % tiles_per_seq, 0, 0)),
                   jax.ShapeDtypeStruct((n_seq, N_HEADS, tiles_per_seq, HEAD_COLS, ROW_TILE), BF16))
        outs[1] =(pl.BlockSpec((None, D_MODEL, ROW_TILE), lambda i: (i // tiles_per_seq, 0, i % tiles_per_seq)),
                   jax.ShapeDtypeStruct((n_seq, D_MODEL, seq_len), F32))
        outs.append(row(BF16))
        kb_per_tile = ROW_TILE // ATTN_TK
        outs.append((pl.BlockSpec((None, N_HEADS, kb_per_tile, VT_ROWS, ATTN_TK),
                                  lambda i: (i // tiles_per_seq, 0, i % tiles_per_seq, 0, 0)),
                     jax.ShapeDtypeStruct((n_seq, N_HEADS, tiles_per_seq * kb_per_tile, VT_ROWS, ATTN_TK), BF16)))
    outs += [row(BF16), row(BF16),
             (seq_block, jax.ShapeDtypeStruct((n_seq, CONV_W - 1, D_MODEL), F32))]
    kern = functools.partial(_inproj_kernel, nb=nb, lt=lt, tiles_per_seq=tiles_per_seq, emit_kv=emit_kv)
    return pl.pallas_call(
        kern,
        grid=(n_tiles,),
        in_specs=[pl.BlockSpec((ROW_TILE, D_MODEL), lambda i: (i, 0)),
                  _resident((1, D_MODEL)), _resident((D_MODEL, 8 * D_MODEL)), _resident((1, 2 * D_MODEL)),
                  _resident((CONV_W, D_MODEL)), _resident((1, D_MODEL)), _resident((D_MODEL, D_MODEL)),
                  seq_block],
        out_specs=[o[0] for o in outs],
        out_shape=[o[1] for o in outs],
        scratch_shapes=[pltpu.VMEM((nb, CONV_W - 1, D_MODEL), F32)],
        compiler_params=_params(1),
        name="inproj",
    )(x, g_pre, w_in, b_gate, conv_w, conv_b, w_conv_out, state)


def _lambda(lq1_ref, lk1_ref, lq2_ref, lk2_ref, lam_init):
    a = jnp.sum(lq1_ref[...] * lk1_ref[...], axis=-1, keepdims=True)
    b = jnp.sum(lq2_ref[...] * lk2_ref[...], axis=-1, keepdims=True)
    return jnp.exp(a) - jnp.exp(b) + lam_init


def _split_maps(qb):
    lane = lax.broadcasted_iota(jnp.int32, qb.shape, 1)
    zero = jnp.zeros_like(qb)
    return jnp.concatenate([jnp.where(lane < HEAD_DIM, qb, zero),
                            jnp.where(lane >= HEAD_DIM, qb, zero)], axis=0)


def _subln(o, g, lam_init):
    return (o * lax.rsqrt(jnp.mean(o * o, axis=-1, keepdims=True) + EPS) * g) * (1.0 - lam_init)


def _prompt_attn_kernel(lq1_ref, lk1_ref, lq2_ref, lk2_ref, sg_ref, q_ref, k_ref, vt_ref, o_ref,
                        qq_scr, s_a, s_b, acc_scr, m_scr, *, lam_init, n_q):
    tq, tk = ATTN_TQ, ATTN_TK
    bufs = (s_a, s_b)
    n_ct = 2 * tq // MXU_COLS
    ct_per_map = tq // MXU_COLS
    blocks_per_tile = tq // tk
    lam = _lambda(lq1_ref, lk1_ref, lq2_ref, lk2_ref, lam_init)

    def build_qq(qi):
        parts = tq // ROW_TILE
        for part in range(parts):
            q_t = q_ref[parts * qi + part]
            row = lax.broadcasted_iota(jnp.int32, q_t.shape, 0)
            zero = jnp.zeros_like(q_t)
            lo = part * ROW_TILE
            qq_scr[:, lo:lo + ROW_TILE] = jnp.where(row < HEAD_DIM, q_t, zero)
            qq_scr[:, tq + lo:tq + lo + ROW_TILE] = jnp.where(row >= HEAD_DIM, q_t, zero)

    def init_state():
        m_scr[...] = jnp.full(m_scr.shape, NEG, F32)
        acc_scr[...] = jnp.zeros(acc_scr.shape, F32)

    def scores(blk, parity, ct):
        cols = slice(ct * MXU_COLS, (ct + 1) * MXU_COLS)
        k_blk = k_ref[pl.ds(pl.multiple_of(blk * tk, tk), tk), :]
        bufs[parity][:, cols] = jnp.dot(k_blk, qq_scr[:, cols], preferred_element_type=F32)

    def softmax(s_in, c, key_offset):
        strip = slice(c, c + 128)
        s = s_in[:, strip]
        if key_offset is not None:
            kc = (lax.broadcasted_iota(jnp.int32, s.shape, 0) + key_offset) // CHUNK
            qc = (lax.broadcasted_iota(jnp.int32, s.shape, 1) + (c % tq)) // CHUNK
            s = jnp.where(kc <= qc, s, NEG)
        m_prev = m_scr[:, strip]
        m_new = jnp.maximum(m_prev, jnp.max(s, axis=0, keepdims=True))
        alpha = jnp.exp2(m_prev - m_new)
        p = jnp.exp2(s - m_new)
        m_scr[:, strip] = m_new
        return p.astype(BF16), alpha

    def attend(blk, parity, ct, key_offset=None):
        c = ct * MXU_COLS
        cols = slice(c, c + MXU_COLS)
        strips = [softmax(bufs[parity], cc, key_offset) for cc in range(c, c + MXU_COLS, 128)]
        p = jnp.concatenate([st[0] for st in strips], axis=1)
        alpha = jnp.concatenate([st[1] for st in strips], axis=1)
        pv = jnp.dot(vt_ref[blk], p, preferred_element_type=F32)
        acc_scr[:, cols] = alpha * acc_scr[:, cols] + pv

    def finalize(qi, part):
        lo = part * MXU_COLS
        o1 = acc_scr[:V_DIM, lo:lo + MXU_COLS] * (1.0 / acc_scr[V_DIM:V_DIM + 1, lo:lo + MXU_COLS])
        o2 = (acc_scr[:V_DIM, tq + lo:tq + lo + MXU_COLS]
              * (1.0 / acc_scr[V_DIM:V_DIM + 1, tq + lo:tq + lo + MXU_COLS]))
        o_t = o1 - lam * o2
        rows = pl.ds(pl.multiple_of(qi * tq + lo, MXU_COLS), MXU_COLS)
        o_ref[rows, :] = _subln(o_t.T, sg_ref[...], lam_init).astype(BF16)

    past_units = [(j, ct) for j in range(blocks_per_tile) for ct in range(n_ct)]

    def past_body(t, carry):
        base = blocks_per_tile * t
        for n, (j, ct) in enumerate(past_units):
            ahead = n + SCORE_LEAD
            j2, ct2 = past_units[ahead % len(past_units)]
            if ahead >= len(past_units):
                j2 += blocks_per_tile
            scores(base + j2, j2 % 2, ct2)
            attend(base + j, j % 2, ct)
        return carry

    own_units = [(j, ct) for j in range(blocks_per_tile) for ct in range(n_ct)
                 if (ct % ct_per_map) * MXU_COLS >= j * tk]

    for n in range(len(own_units)):
        ahead = n + SCORE_LEAD
        j2, ct2 = own_units[ahead] if ahead < len(own_units) else (0, ahead - len(own_units))
        readers = [i for i, (j, ct) in enumerate(own_units) if ct == ct2 and j % 2 == j2 % 2 and i < ahead]
        assert all(i < n for i in readers), (n, j2, ct2, readers)

    def tile_body(qi, carry):
        lax.fori_loop(0, qi, past_body, 0)
        base = blocks_per_tile * qi
        next_qi = jnp.minimum(qi + 1, n_q - 1)
        for n, (j, ct) in enumerate(own_units):
            ahead = n + SCORE_LEAD
            if ahead < len(own_units):
                j2, ct2 = own_units[ahead]
                scores(base + j2, j2 % 2, ct2)
            else:
                if ahead == len(own_units):
                    build_qq(next_qi)
                scores(0, 0, ahead - len(own_units))
            on_diagonal = (ct % ct_per_map) * MXU_COLS < (j + 1) * tk
            attend(base + j, j % 2, ct, key_offset=j * tk if on_diagonal else None)
            if on_diagonal and ct >= ct_per_map:
                finalize(qi, ct % ct_per_map)
        init_state()
        return carry

    build_qq(0)
    init_state()
    for ct in range(SCORE_LEAD):
        scores(0, 0, ct)
    lax.fori_loop(0, n_q, tile_body, 0)


def _prompt_attn(lq1, lk1, lq2, lk2, subln_g, q, kb, vt, *, n_seq, seq_len, lam_init):
    t = n_seq * seq_len
    nq = seq_len // ATTN_TQ
    nk = seq_len // ATTN_TK
    vec = _resident((1, HEAD_DIM))
    stat = pltpu.VMEM((1, 2 * ATTN_TQ), F32)
    score = pltpu.VMEM((ATTN_TK, 2 * ATTN_TQ), F32)
    head_rows = pl.BlockSpec((seq_len, HEAD_COLS), lambda b, h: (b, h))
    return pl.pallas_call(
        functools.partial(_prompt_attn_kernel, lam_init=lam_init, n_q=nq),
        grid=(n_seq, N_HEADS),
        in_specs=[vec, vec, vec, vec, _resident((1, V_DIM)),
                  pl.BlockSpec((None, None, seq_len // ROW_TILE, HEAD_COLS, ROW_TILE), lambda b, h: (b, h, 0, 0, 0)),
                  head_rows,
                  pl.BlockSpec((None, None, nk, VT_ROWS, ATTN_TK), lambda b, h: (b, h, 0, 0, 0))],
        out_specs=head_rows,
        out_shape=jax.ShapeDtypeStruct((t, N_HEADS * V_DIM), BF16),
        scratch_shapes=[pltpu.VMEM((HEAD_COLS, 2 * ATTN_TQ), BF16), score, score,
                        pltpu.VMEM((VT_ROWS, 2 * ATTN_TQ), F32), stat],
        compiler_params=_params(2),
        name="prompt_attn",
    )(lq1, lk1, lq2, lk2, subln_g, q, kb, vt)


def _sample_attn_kernel(lq1_ref, lk1_ref, lq2_ref, lk2_ref, sg_ref, q_ref, kn_ref, vn_ref, ck_ref, cv_ref,
                        o_ref, *, lam_init):
    tq = q_ref.shape[0]
    past = ck_ref.shape[1]
    lam = _lambda(lq1_ref, lk1_ref, lq2_ref, lk2_ref, lam_init)
    nt = (((1,), (1,)), ((), ()))
    n_chunks = past // SAMPLE_TK
    for hh in range(N_HEADS):
        cols = slice(hh * HEAD_COLS, (hh + 1) * HEAD_COLS)
        qq = _split_maps(q_ref[:, cols])

        def cache_scores(i):
            keys = slice(i * SAMPLE_TK, (i + 1) * SAMPLE_TK)
            return jnp.dot(qq, ck_ref[cols, keys].astype(BF16), preferred_element_type=F32)

        s_n = lax.dot_general(qq, kn_ref[:, cols].astype(BF16), nt, preferred_element_type=F32)
        running = cache_scores(0)
        for i in range(1, n_chunks):
            running = jnp.maximum(running, cache_scores(i))
        m = jnp.maximum(jnp.max(running, axis=-1, keepdims=True), jnp.max(s_n, axis=-1, keepdims=True))
        p_n = jnp.exp2(s_n - m)
        acc = jnp.dot(p_n.astype(BF16), vn_ref[:, cols].astype(BF16), preferred_element_type=F32)
        partial = jnp.zeros((2 * tq, SAMPLE_TK), F32)
        for i in range(n_chunks):
            p = jnp.exp2(cache_scores(i) - m)
            partial = partial + p
            v_c = cv_ref[pl.ds(hh + i * SAMPLE_TK * N_HEADS, SAMPLE_TK, stride=N_HEADS), :]
            acc = acc + jnp.dot(p.astype(BF16), v_c.astype(BF16), preferred_element_type=F32)
        denom = jnp.sum(partial, axis=-1, keepdims=True) + jnp.sum(p_n, axis=-1, keepdims=True)
        normed = acc * (1.0 / denom)
        o = normed[:tq] - lam * normed[tq:]
        o_ref[:, cols] = _subln(o, sg_ref[...], lam_init).astype(BF16)


def _sample_attn(lq1, lk1, lq2, lk2, subln_g, q, k_new, v_new, cache_k, cache_v, *, n_seq, seq_len, lam_init):
    past = cache_k.shape[2]
    vec = _resident((1, HEAD_DIM))
    new = pl.BlockSpec((seq_len, D_MODEL), lambda b: (b, 0))
    old_k = pl.BlockSpec((None, D_MODEL, past), lambda b: (b, 0, 0))
    old_v = pl.BlockSpec((None, past * N_HEADS, V_DIM), lambda b: (b, 0, 0))
    return pl.pallas_call(
        functools.partial(_sample_attn_kernel, lam_init=lam_init),
        grid=(n_seq,),
        in_specs=[vec, vec, vec, vec, _resident((1, V_DIM)), new, new, new, old_k, old_v],
        out_specs=new,
        out_shape=jax.ShapeDtypeStruct((n_seq * seq_len, D_MODEL), BF16),
        compiler_params=_params(1),
        name="sample_attn",
    )(lq1, lk1, lq2, lk2, subln_g, q, k_new, v_new, cache_k, cache_v)


def kernel(x_prompt, x_sample, cache_k, cache_v, state_conv, ffn1_norm_pre, ffn1_norm_post, ffn1_w_gate, ffn1_w_up, ffn1_w_down, mix_norm_pre, mix_norm_post, w_in, b_gate, lambda_q1, lambda_k1, lambda_q2, lambda_k2, subln_g, w_attn_out, conv_w, conv_b, w_conv_out, w_mix_out, ffn2_norm_pre, ffn2_norm_post, ffn2_w_gate, ffn2_w_up, ffn2_w_down):
    depth = w_in.shape[0]
    n_p, s_p, _ = x_prompt.shape
    n_s, s_s, _ = x_sample.shape
    past = cache_k.shape[2]
    assert s_p % ROW_TILE == 0 and ROW_TILE % ATTN_TK == 0 and ATTN_TK % CHUNK == 0
    assert s_p % ATTN_TQ == 0 and ATTN_TQ % (2 * ATTN_TK) == 0 and SCORE_LEAD <= ATTN_TQ // MXU_COLS
    assert ATTN_TK == MXU_COLS
    assert ROW_TILE % s_s == 0 and (n_s * s_s) % ROW_TILE == 0 and s_s >= CONV_W - 1 and past % SAMPLE_TK == 0

    xp = x_prompt.reshape(n_p * s_p, D_MODEL)
    xs = x_sample.reshape(n_s * s_s, D_MODEL)
    outs = {name: [] for name in ("kp", "vp", "cp", "ks", "vs", "cs")}
    for l in range(depth):
        lam_init = _lambda_init(l)
        vec = lambda a: a[l][None, :]
        bf = lambda a: a[l].astype(BF16)
        ffn1 = (vec(ffn1_norm_pre), vec(ffn1_norm_post), bf(ffn1_w_gate), bf(ffn1_w_up), bf(ffn1_w_down))
        ffn2 = (vec(ffn2_norm_pre), vec(ffn2_norm_post), bf(ffn2_w_gate), bf(ffn2_w_up), bf(ffn2_w_down))
        proj = (vec(mix_norm_pre), bf(w_in), vec(b_gate), conv_w[l], vec(conv_b), bf(w_conv_out))
        lam = (vec(lambda_q1), vec(lambda_k1), vec(lambda_q2), vec(lambda_k2), vec(subln_g))
        mix_w = (bf(w_attn_out), bf(w_mix_out), vec(mix_norm_post))

        xp = _ffn(xp, *ffn1)
        zero_state = jnp.zeros((n_p, CONV_W - 1, D_MODEL), F32)
        q, k, v, kb, vt, ga, gy, conv = _inproj(xp, *proj, zero_state, seq_len=s_p, emit_kv=True)
        o = _prompt_attn(*lam, q, kb, vt, n_seq=n_p, seq_len=s_p, lam_init=lam_init)
        xp = _mix_ffn(o, ga, gy, xp, *mix_w, *ffn2)
        outs["kp"].append(jnp.transpose(k.reshape(n_p, N_HEADS, 2, HEAD_DIM, s_p), (0, 4, 1, 2, 3)))
        outs["vp"].append(v.reshape(n_p, s_p, N_HEADS, V_DIM))
        outs["cp"].append(conv)

        xs = _ffn(xs, *ffn1)
        q, k, v, ga, gy, conv = _inproj(xs, *proj, state_conv[l], seq_len=s_s, emit_kv=False)
        cache_kt = jnp.transpose(cache_k[l], (0, 2, 3, 4, 1)).reshape(n_s, D_MODEL, past)
        cache_vr = cache_v[l].reshape(n_s, past * N_HEADS, V_DIM)
        o = _sample_attn(*lam, q, k, v, cache_kt, cache_vr, n_seq=n_s, seq_len=s_s, lam_init=lam_init)
        xs = _mix_ffn(o, ga, gy, xs, *mix_w, *ffn2)
        outs["ks"].append(k.reshape(n_s, s_s, N_HEADS, 2, HEAD_DIM))
        outs["vs"].append(v.reshape(n_s, s_s, N_HEADS, V_DIM))
        outs["cs"].append(conv)

    return (xp.reshape(n_p, s_p, D_MODEL), xs.reshape(n_s, s_s, D_MODEL),
            jnp.stack(outs["kp"]), jnp.stack(outs["vp"]), jnp.stack(outs["cp"]),
            jnp.stack(outs["ks"]), jnp.stack(outs["vs"]), jnp.stack(outs["cs"]))
```

```python
import functools
import math

import jax
import jax.numpy as jnp
from jax import lax
from jax.experimental import pallas as pl
from jax.experimental.pallas import tpu as pltpu

D_MODEL = 1024
N_HEADS = 8
HEAD_DIM = 64
V_DIM = 2 * HEAD_DIM
HEAD_COLS = 2 * HEAD_DIM
D_FF = 2816
CHUNK = 64
CONV_W = 3
EPS = 1e-6
NEG = -1e30
LOG2E = 1.4426950408889634
Q_SCALE = (HEAD_DIM ** -0.5) * LOG2E

V7X_VMEM_LIMIT_BYTES = 58 * 1024 * 1024
ROW_TILE = 512
MXU_COLS = 256
FF_CHUNK = MXU_COLS
ATTN_TQ = 2048
ATTN_TK = 256
SCORE_LEAD = 4
SAMPLE_TK = 128
BF16_SUBLANES = 16
VT_ROWS = V_DIM + BF16_SUBLANES

F32 = jnp.float32
BF16 = jnp.bfloat16


def _lambda_init(layer_idx):
    return 0.8 - 0.6 * math.exp(-0.3 * layer_idx)


def _rms(x, g):
    return x * lax.rsqrt(jnp.mean(x * x, axis=-1, keepdims=True) + EPS) * g


def _resident(shape):
    zeros = (0,) * len(shape)
    return pl.BlockSpec(shape, lambda *_: zeros, pipeline_mode=pl.Buffered(1))


def _params(n_axes):
    return pltpu.CompilerParams(
        dimension_semantics=("arbitrary",) * n_axes,
        vmem_limit_bytes=V7X_VMEM_LIMIT_BYTES,
    )


def _swiglu_residual(x, gpre_ref, gpost_ref, wg_ref, wu_ref, wd_ref, a_scr):
    h = _rms(x, gpre_ref[...]).astype(BF16)
    for c in range(0, D_FF, FF_CHUNK):
        cols = slice(c, min(c + FF_CHUNK, D_FF))
        g = jnp.dot(h, wg_ref[:, cols], preferred_element_type=F32)
        u = jnp.dot(h, wu_ref[:, cols], preferred_element_type=F32)
        a_scr[:, cols] = (g * jax.nn.sigmoid(g) * u).astype(BF16)
    y = jnp.dot(a_scr[...], wd_ref[...], preferred_element_type=F32)
    return x + 0.5 * _rms(y, gpost_ref[...])


def _ffn_kernel(x_ref, gpre_ref, gpost_ref, wg_ref, wu_ref, wd_ref, o_ref, a_scr):
    o_ref[...] = _swiglu_residual(x_ref[...], gpre_ref, gpost_ref, wg_ref, wu_ref, wd_ref, a_scr)


def _mix_ffn_kernel(o_ref, ga_ref, gy_ref, x_ref, wa_ref, wm_ref, gmix_ref,
                    gpre_ref, gpost_ref, wg_ref, wu_ref, wd_ref, out_ref, a_scr):
    y_attn = jnp.dot(o_ref[...], wa_ref[...], preferred_element_type=F32)
    merged = (ga_ref[...].astype(F32) * y_attn + gy_ref[...].astype(F32)).astype(BF16)
    y = jnp.dot(merged, wm_ref[...], preferred_element_type=F32)
    x = x_ref[...] + _rms(y, gmix_ref[...])
    out_ref[...] = _swiglu_residual(x, gpre_ref, gpost_ref, wg_ref, wu_ref, wd_ref, a_scr)


_FFN_WEIGHT_SPECS = (((1, D_MODEL)), ((1, D_MODEL)), (D_MODEL, D_FF), (D_MODEL, D_FF), (D_FF, D_MODEL))


def _ffn(x, g_pre, g_post, w_gate, w_up, w_down):
    t = x.shape[0]
    row = pl.BlockSpec((ROW_TILE, D_MODEL), lambda i: (i, 0))
    return pl.pallas_call(
        _ffn_kernel,
        grid=(t // ROW_TILE,),
        in_specs=[row] + [_resident(s) for s in _FFN_WEIGHT_SPECS],
        out_specs=row,
        out_shape=jax.ShapeDtypeStruct((t, D_MODEL), F32),
        scratch_shapes=[pltpu.VMEM((ROW_TILE, D_FF), BF16)],
        compiler_params=_params(1),
        name="ffn",
    )(x, g_pre, g_post, w_gate, w_up, w_down)


def _mix_ffn(o, gate_a, gated_conv, x, w_attn_out, w_mix_out, g_mix, g_pre, g_post, w_gate, w_up, w_down):
    t = x.shape[0]
    row = pl.BlockSpec((ROW_TILE, D_MODEL), lambda i: (i, 0))
    return pl.pallas_call(
        _mix_ffn_kernel,
        grid=(t // ROW_TILE,),
        in_specs=[row, row, row, row, _resident((D_MODEL, D_MODEL)), _resident((D_MODEL, D_MODEL)),
                  _resident((1, D_MODEL))] + [_resident(s) for s in _FFN_WEIGHT_SPECS],
        out_specs=row,
        out_shape=jax.ShapeDtypeStruct((t, D_MODEL), F32),
        scratch_shapes=[pltpu.VMEM((ROW_TILE, D_FF), BF16)],
        compiler_params=_params(1),
        name="mix_ffn",
    )(o, gate_a, gated_conv, x, w_attn_out, w_mix_out, g_mix, g_pre, g_post, w_gate, w_up, w_down)


def _inproj_kernel(x_ref, gpre_ref, w_ref, bg_ref, cw_ref, cb_ref, wco_ref, st_ref, *refs,
                   nb, lt, tiles_per_seq, emit_kv):
    if emit_kv:
        q_ref, k_ref, v_ref, kb_ref, vt_ref, ga_ref, gy_ref, nc_ref, carry_scr = refs
    else:
        q_ref, k_ref, v_ref, ga_ref, gy_ref, nc_ref, carry_scr = refs
    i = pl.program_id(0)
    h = _rms(x_ref[...], gpre_ref[...]).astype(BF16)

    def seg(j):
        return jnp.dot(h, w_ref[:, j * D_MODEL:(j + 1) * D_MODEL], preferred_element_type=F32)

    q = seg(0) * Q_SCALE
    if emit_kv:
        for hh in range(N_HEADS):
            q_ref[hh] = q[:, hh * HEAD_COLS:(hh + 1) * HEAD_COLS].T.astype(BF16)
    else:
        q_ref[...] = q.astype(BF16)
    k = seg(1)
    k_ref[...] = k.T if emit_kv else k
    v = seg(2)
    v_ref[...] = v
    if emit_kv:
        kb_ref[...] = k.astype(BF16)
        for hh in range(N_HEADS):
            for kb in range(ROW_TILE // ATTN_TK):
                vt_ref[hh, kb, :V_DIM, :] = (
                    v[kb * ATTN_TK:(kb + 1) * ATTN_TK, hh * V_DIM:(hh + 1) * V_DIM].T.astype(BF16))
                vt_ref[hh, kb, V_DIM:, :] = jnp.ones((BF16_SUBLANES, ATTN_TK), BF16)

    @pl.when(i % tiles_per_seq == 0)
    def _():
        carry_scr[...] = st_ref[...]

    gate_b = seg(3).reshape(nb, lt, D_MODEL)
    u = (seg(4) * seg(5)).reshape(nb, lt, D_MODEL)
    prev = carry_scr[...]
    prev0 = prev[:, 0:1, :]
    prev1 = prev[:, 1:2, :]
    pos = lax.broadcasted_iota(jnp.int32, u.shape, 1)
    back1 = jnp.where(pos == 0, prev1, pltpu.roll(u, 1, 1))
    back2 = jnp.where(pos == 0, prev0, jnp.where(pos == 1, prev1, pltpu.roll(u, 2, 1)))
    cw = cw_ref[...]
    zc = cb_ref[...] + cw[0:1, :] * back2 + cw[1:2, :] * back1 + cw[2:3, :] * u
    tail = u[:, lt - (CONV_W - 1):, :]
    carry_scr[...] = tail
    nc_ref[...] = tail
    c_in = (gate_b * zc).reshape(nb * lt, D_MODEL).astype(BF16)
    y_conv = jnp.dot(c_in, wco_ref[...], preferred_element_type=F32)

    bg = bg_ref[...]
    ga_ref[...] = jax.nn.sigmoid(seg(6) + bg[:, :D_MODEL]).astype(BF16)
    gy_ref[...] = (jax.nn.sigmoid(seg(7) + bg[:, D_MODEL:]) * y_conv).astype(BF16)


def _inproj(x, g_pre, w_in, b_gate, conv_w, conv_b, w_conv_out, state, *, seq_len, emit_kv):
    t = x.shape[0]
    n_seq = t // seq_len
    lt = min(seq_len, ROW_TILE)
    nb = ROW_TILE // lt
    tiles_per_seq = seq_len // lt
    n_tiles = t // ROW_TILE
    row = lambda dt: (pl.BlockSpec((ROW_TILE, D_MODEL), lambda i: (i, 0)),
                      jax.ShapeDtypeStruct((t, D_MODEL), dt))
    seq_block = pl.BlockSpec((nb, CONV_W - 1, D_MODEL), lambda i: (i // tiles_per_seq, 0, 0))
    outs = [row(BF16), row(F32), row(F32)]
    if emit_kv:
        outs[0] = (pl.BlockSpec((None, N_HEADS, None, HEAD_COLS, ROW_TILE),
                                lambda i: (i // tiles_per_seq, 0, i % tiles_per_seq, 0, 0)),
                   jax.ShapeDtypeStruct((n_seq, N_HEADS, tiles_per_seq, HEAD_COLS, ROW_TILE), BF16))
        outs[1] =(pl.BlockSpec((None, D_MODEL, ROW_TILE), lambda i: (i // tiles_per_seq, 0, i % tiles_per_seq)),
                   jax.ShapeDtypeStruct((n_seq, D_MODEL, seq_len), F32))
        outs.append(row(BF16))
        kb_per_tile = ROW_TILE // ATTN_TK
        outs.append((pl.BlockSpec((None, N_HEADS, kb_per_tile, VT_ROWS, ATTN_TK),
                                  lambda i: (i // tiles_per_seq, 0, i % tiles_per_seq, 0, 0)),
                     jax.ShapeDtypeStruct((n_seq, N_HEADS, tiles_per_seq * kb_per_tile, VT_ROWS, ATTN_TK), BF16)))
    outs += [row(BF16), row(BF16),
             (seq_block, jax.ShapeDtypeStruct((n_seq, CONV_W - 1, D_MODEL), F32))]
    kern = functools.partial(_inproj_kernel, nb=nb, lt=lt, tiles_per_seq=tiles_per_seq, emit_kv=emit_kv)
    return pl.pallas_call(
        kern,
        grid=(n_tiles,),
        in_specs=[pl.BlockSpec((ROW_TILE, D_MODEL), lambda i: (i, 0)),
                  _resident((1, D_MODEL)), _resident((D_MODEL, 8 * D_MODEL)), _resident((1, 2 * D_MODEL)),
                  _resident((CONV_W, D_MODEL)), _resident((1, D_MODEL)), _resident((D_MODEL, D_MODEL)),
                  seq_block],
        out_specs=[o[0] for o in outs],
        out_shape=[o[1] for o in outs],
        scratch_shapes=[pltpu.VMEM((nb, CONV_W - 1, D_MODEL), F32)],
        compiler_params=_params(1),
        name="inproj",
    )(x, g_pre, w_in, b_gate, conv_w, conv_b, w_conv_out, state)


def _lambda(lq1_ref, lk1_ref, lq2_ref, lk2_ref, lam_init):
    a = jnp.sum(lq1_ref[...] * lk1_ref[...], axis=-1, keepdims=True)
    b = jnp.sum(lq2_ref[...] * lk2_ref[...], axis=-1, keepdims=True)
    return jnp.exp(a) - jnp.exp(b) + lam_init


def _split_maps(qb):
    lane = lax.broadcasted_iota(jnp.int32, qb.shape, 1)
    zero = jnp.zeros_like(qb)
    return jnp.concatenate([jnp.where(lane < HEAD_DIM, qb, zero),
                            jnp.where(lane >= HEAD_DIM, qb, zero)], axis=0)


def _subln(o, g, lam_init):
    return (o * lax.rsqrt(jnp.mean(o * o, axis=-1, keepdims=True) + EPS) * g) * (1.0 - lam_init)


def _prompt_attn_kernel(lq1_ref, lk1_ref, lq2_ref, lk2_ref, sg_ref, q_ref, k_ref, vt_ref, o_ref,
                        qq_scr, s_a, s_b, acc_scr, m_scr, *, lam_init, n_q):
    tq, tk = ATTN_TQ, ATTN_TK
    bufs = (s_a, s_b)
    n_ct = 2 * tq // MXU_COLS
    ct_per_map = tq // MXU_COLS
    blocks_per_tile = tq // tk
    lam = _lambda(lq1_ref, lk1_ref, lq2_ref, lk2_ref, lam_init)

    def build_qq(qi):
        parts = tq // ROW_TILE
        for part in range(parts):
            q_t = q_ref[parts * qi + part]
            row = lax.broadcasted_iota(jnp.int32, q_t.shape, 0)
            zero = jnp.zeros_like(q_t)
            lo = part * ROW_TILE
            qq_scr[:, lo:lo + ROW_TILE] = jnp.where(row < HEAD_DIM, q_t, zero)
            qq_scr[:, tq + lo:tq + lo + ROW_TILE] = jnp.where(row >= HEAD_DIM, q_t, zero)

    def init_state():
        m_scr[...] = jnp.full(m_scr.shape, NEG, F32)
        acc_scr[...] = jnp.zeros(acc_scr.shape, F32)

    def scores(blk, parity, ct):
        cols = slice(ct * MXU_COLS, (ct + 1) * MXU_COLS)
        k_blk = k_ref[pl.ds(pl.multiple_of(blk * tk, tk), tk), :]
        bufs[parity][:, cols] = jnp.dot(k_blk, qq_scr[:, cols], preferred_element_type=F32)

    def softmax(s_in, c, key_offset):
        strip = slice(c, c + 128)
        s = s_in[:, strip]
        if key_offset is not None:
            kc = (lax.broadcasted_iota(jnp.int32, s.shape, 0) + key_offset) // CHUNK
            qc = (lax.broadcasted_iota(jnp.int32, s.shape, 1) + (c % tq)) // CHUNK
            s = jnp.where(kc <= qc, s, NEG)
        m_prev = m_scr[:, strip]
        m_new = jnp.maximum(m_prev, jnp.max(s, axis=0, keepdims=True))
        alpha = jnp.exp2(m_prev - m_new)
        p = jnp.exp2(s - m_new)
        m_scr[:, strip] = m_new
        return p.astype(BF16), alpha

    def attend(blk, parity, ct, key_offset=None):
        c = ct * MXU_COLS
        cols = slice(c, c + MXU_COLS)
        strips = [softmax(bufs[parity], cc, key_offset) for cc in range(c, c + MXU_COLS, 128)]
        p = jnp.concatenate([st[0] for st in strips], axis=1)
        alpha = jnp.concatenate([st[1] for st in strips], axis=1)
        pv = jnp.dot(vt_ref[blk], p, preferred_element_type=F32)
        acc_scr[:, cols] = alpha * acc_scr[:, cols] + pv

    def finalize(qi, part):
        lo = part * MXU_COLS
        o1 = acc_scr[:V_DIM, lo:lo + MXU_COLS] * (1.0 / acc_scr[V_DIM:V_DIM + 1, lo:lo + MXU_COLS])
        o2 = (acc_scr[:V_DIM, tq + lo:tq + lo + MXU_COLS]
              * (1.0 / acc_scr[V_DIM:V_DIM + 1, tq + lo:tq + lo + MXU_COLS]))
        o_t = o1 - lam * o2
        rows = pl.ds(pl.multiple_of(qi * tq + lo, MXU_COLS), MXU_COLS)
        o_ref[rows, :] = _subln(o_t.T, sg_ref[...], lam_init).astype(BF16)

    past_units = [(j, ct) for j in range(blocks_per_tile) for ct in range(n_ct)]

    def past_body(t, carry):
        base = blocks_per_tile * t
        for n, (j, ct) in enumerate(past_units):
            ahead = n + SCORE_LEAD
            j2, ct2 = past_units[ahead % len(past_units)]
            if ahead >= len(past_units):
                j2 += blocks_per_tile
            scores(base + j2, j2 % 2, ct2)
            attend(base + j, j % 2, ct)
        return carry

    own_units = [(j, ct) for j in range(blocks_per_tile) for ct in range(n_ct)
                 if (ct % ct_per_map) * MXU_COLS >= j * tk]

    for n in range(len(own_units)):
        ahead = n + SCORE_LEAD
        j2, ct2 = own_units[ahead] if ahead < len(own_units) else (0, ahead - len(own_units))
        readers = [i for i, (j, ct) in enumerate(own_units) if ct == ct2 and j % 2 == j2 % 2 and i < ahead]
        assert all(i < n for i in readers), (n, j2, ct2, readers)

    def tile_body(qi, carry):
        lax.fori_loop(0, qi, past_body, 0)
        base = blocks_per_tile * qi
        next_qi = jnp.minimum(qi + 1, n_q - 1)
        for n, (j, ct) in enumerate(own_units):
            ahead = n + SCORE_LEAD
            if ahead < len(own_units):
                j2, ct2 = own_units[ahead]
                scores(base + j2, j2 % 2, ct2)
            else:
                if ahead == len(own_units):
                    build_qq(next_qi)
                scores(0, 0, ahead - len(own_units))
            on_diagonal = (ct % ct_per_map) * MXU_COLS < (j + 1) * tk
            attend(base + j, j % 2, ct, key_offset=j * tk if on_diagonal else None)
            if on_diagonal and ct >= ct_per_map:
                finalize(qi, ct % ct_per_map)
        init_state()
        return carry

    build_qq(0)
    init_state()
    for ct in range(SCORE_LEAD):
        scores(0, 0, ct)
    lax.fori_loop(0, n_q, tile_body, 0)


def _prompt_attn(lq1, lk1, lq2, lk2, subln_g, q, kb, vt, *, n_seq, seq_len, lam_init):
    t = n_seq * seq_len
    nq = seq_len // ATTN_TQ
    nk = seq_len // ATTN_TK
    vec = _resident((1, HEAD_DIM))
    stat = pltpu.VMEM((1, 2 * ATTN_TQ), F32)
    score = pltpu.VMEM((ATTN_TK, 2 * ATTN_TQ), F32)
    head_rows = pl.BlockSpec((seq_len, HEAD_COLS), lambda b, h: (b, h))
    return pl.pallas_call(
        functools.partial(_prompt_attn_kernel, lam_init=lam_init, n_q=nq),
        grid=(n_seq, N_HEADS),
        in_specs=[vec, vec, vec, vec, _resident((1, V_DIM)),
                  pl.BlockSpec((None, None, seq_len // ROW_TILE, HEAD_COLS, ROW_TILE), lambda b, h: (b, h, 0, 0, 0)),
                  head_rows,
                  pl.BlockSpec((None, None, nk, VT_ROWS, ATTN_TK), lambda b, h: (b, h, 0, 0, 0))],
        out_specs=head_rows,
        out_shape=jax.ShapeDtypeStruct((t, N_HEADS * V_DIM), BF16),
        scratch_shapes=[pltpu.VMEM((HEAD_COLS, 2 * ATTN_TQ), BF16), score, score,
                        pltpu.VMEM((VT_ROWS, 2 * ATTN_TQ), F32), stat],
        compiler_params=_params(2),
        name="prompt_attn",
    )(lq1, lk1, lq2, lk2, subln_g, q, kb, vt)


def _sample_attn_kernel(lq1_ref, lk1_ref, lq2_ref, lk2_ref, sg_ref, q_ref, kn_ref, vn_ref, ck_ref, cv_ref,
                        o_ref, *, lam_init):
    tq = q_ref.shape[0]
    past = ck_ref.shape[1]
    lam = _lambda(lq1_ref, lk1_ref, lq2_ref, lk2_ref, lam_init)
    nt = (((1,), (1,)), ((), ()))
    n_chunks = past // SAMPLE_TK
    for hh in range(N_HEADS):
        cols = slice(hh * HEAD_COLS, (hh + 1) * HEAD_COLS)
        qq = _split_maps(q_ref[:, cols])

        def cache_scores(i):
            keys = slice(i * SAMPLE_TK, (i + 1) * SAMPLE_TK)
            return jnp.dot(qq, ck_ref[cols, keys].astype(BF16), preferred_element_type=F32)

        s_n = lax.dot_general(qq, kn_ref[:, cols].astype(BF16), nt, preferred_element_type=F32)
        running = cache_scores(0)
        for i in range(1, n_chunks):
            running = jnp.maximum(running, cache_scores(i))
        m = jnp.maximum(jnp.max(running, axis=-1, keepdims=True), jnp.max(s_n, axis=-1, keepdims=True))
        p_n = jnp.exp2(s_n - m)
        acc = jnp.dot(p_n.astype(BF16), vn_ref[:, cols].astype(BF16), preferred_element_type=F32)
        partial = jnp.zeros((2 * tq, SAMPLE_TK), F32)
        for i in range(n_chunks):
            p = jnp.exp2(cache_scores(i) - m)
            partial = partial + p
            v_c = cv_ref[pl.ds(hh + i * SAMPLE_TK * N_HEADS, SAMPLE_TK, stride=N_HEADS), :]
            acc = acc + jnp.dot(p.astype(BF16), v_c.astype(BF16), preferred_element_type=F32)
        denom = jnp.sum(partial, axis=-1, keepdims=True) + jnp.sum(p_n, axis=-1, keepdims=True)
        normed = acc * (1.0 / denom)
        o = normed[:tq] - lam * normed[tq:]
        o_ref[:, cols] = _subln(o, sg_ref[...], lam_init).astype(BF16)


def _sample_attn(lq1, lk1, lq2, lk2, subln_g, q, k_new, v_new, cache_k, cache_v, *, n_seq, seq_len, lam_init):
    past = cache_k.shape[2]
    vec = _resident((1, HEAD_DIM))
    new = pl.BlockSpec((seq_len, D_MODEL), lambda b: (b, 0))
    old_k = pl.BlockSpec((None, D_MODEL, past), lambda b: (b, 0, 0))
    old_v = pl.BlockSpec((None, past * N_HEADS, V_DIM), lambda b: (b, 0, 0))
    return pl.pallas_call(
        functools.partial(_sample_attn_kernel, lam_init=lam_init),
        grid=(n_seq,),
        in_specs=[vec, vec, vec, vec, _resident((1, V_DIM)), new, new, new, old_k, old_v],
        out_specs=new,
        out_shape=jax.ShapeDtypeStruct((n_seq * seq_len, D_MODEL), BF16),
        compiler_params=_params(1),
        name="sample_attn",
    )(lq1, lk1, lq2, lk2, subln_g, q, k_new, v_new, cache_k, cache_v)


def kernel(x_prompt, x_sample, cache_k, cache_v, state_conv, ffn1_norm_pre, ffn1_norm_post, ffn1_w_gate, ffn1_w_up, ffn1_w_down, mix_norm_pre, mix_norm_post, w_in, b_gate, lambda_q1, lambda_k1, lambda_q2, lambda_k2, subln_g, w_attn_out, conv_w, conv_b, w_conv_out, w_mix_out, ffn2_norm_pre, ffn2_norm_post, ffn2_w_gate, ffn2_w_up, ffn2_w_down):
    depth = w_in.shape[0]
    n_p, s_p, _ = x_prompt.shape
    n_s, s_s, _ = x_sample.shape
    past = cache_k.shape[2]
    assert s_p % ROW_TILE == 0 and ROW_TILE % ATTN_TK == 0 and ATTN_TK % CHUNK == 0
    assert s_p % ATTN_TQ == 0 and ATTN_TQ % (2 * ATTN_TK) == 0 and SCORE_LEAD <= ATTN_TQ // MXU_COLS
    assert ATTN_TK == MXU_COLS
    assert ROW_TILE % s_s == 0 and (n_s * s_s) % ROW_TILE == 0 and s_s >= CONV_W - 1 and past % SAMPLE_TK == 0

    xp = x_prompt.reshape(n_p * s_p, D_MODEL)
    xs = x_sample.reshape(n_s * s_s, D_MODEL)
    outs = {name: [] for name in ("kp", "vp", "cp", "ks", "vs", "cs")}
    for l in range(depth):
        lam_init = _lambda_init(l)
        vec = lambda a: a[l][None, :]
        bf = lambda a: a[l].astype(BF16)
        ffn1 = (vec(ffn1_norm_pre), vec(ffn1_norm_post), bf(ffn1_w_gate), bf(ffn1_w_up), bf(ffn1_w_down))
        ffn2 = (vec(ffn2_norm_pre), vec(ffn2_norm_post), bf(ffn2_w_gate), bf(ffn2_w_up), bf(ffn2_w_down))
        proj = (vec(mix_norm_pre), bf(w_in), vec(b_gate), conv_w[l], vec(conv_b), bf(w_conv_out))
        lam = (vec(lambda_q1), vec(lambda_k1), vec(lambda_q2), vec(lambda_k2), vec(subln_g))
        mix_w = (bf(w_attn_out), bf(w_mix_out), vec(mix_norm_post))

        xp = _ffn(xp, *ffn1)
        zero_state = jnp.zeros((n_p, CONV_W - 1, D_MODEL), F32)
        q, k, v, kb, vt, ga, gy, conv = _inproj(xp, *proj, zero_state, seq_len=s_p, emit_kv=True)
        o = _prompt_attn(*lam, q, kb, vt, n_seq=n_p, seq_len=s_p, lam_init=lam_init)
        xp = _mix_ffn(o, ga, gy, xp, *mix_w, *ffn2)
        outs["kp"].append(jnp.transpose(k.reshape(n_p, N_HEADS, 2, HEAD_DIM, s_p), (0, 4, 1, 2, 3)))
        outs["vp"].append(v.reshape(n_p, s_p, N_HEADS, V_DIM))
        outs["cp"].append(conv)

        xs = _ffn(xs, *ffn1)
        q, k, v, ga, gy, conv = _inproj(xs, *proj, state_conv[l], seq_len=s_s, emit_kv=False)
        cache_kt = jnp.transpose(cache_k[l], (0, 2, 3, 4, 1)).reshape(n_s, D_MODEL, past)
        cache_vr = cache_v[l].reshape(n_s, past * N_HEADS, V_DIM)
        o = _sample_attn(*lam, q, k, v, cache_kt, cache_vr, n_seq=n_s, seq_len=s_s, lam_init=lam_init)
        xs = _mix_ffn(o, ga, gy, xs, *mix_w, *ffn2)
        outs["ks"].append(k.reshape(n_s, s_s, N_HEADS, 2, HEAD_DIM))
        outs["vs"].append(v.reshape(n_s, s_s, N_HEADS, V_DIM))
        outs["cs"].append(conv)

    return (xp.reshape(n_p, s_p, D_MODEL), xs.reshape(n_s, s_s, D_MODEL),
            jnp.stack(outs["kp"]), jnp.stack(outs["vp"]), jnp.stack(outs["cp"]),
            jnp.stack(outs["ks"]), jnp.stack(outs["vs"]), jnp.stack(outs["cs"]))
```
